```python
import jax, jax.numpy as jnp
from jax import lax
import numpy as np

D_MODEL = 1024
BATCH = 16
SEQ = 2048
DEPTH = 2

N_EVEN = (DEPTH + 1) // 2
N_ODD = DEPTH // 2
D_A = D_MODEL // 2
CONV_A_WIDTH = 31
D_B = D_MODEL // 2
CONV_B_WIDTH = 3
CONV_IN = 2 * D_A + 3 * D_B
GLA_HEADS = 4
GLA_DK = D_MODEL // 2 // GLA_HEADS
GLA_DV = D_MODEL // GLA_HEADS
GATE_RANK = 16
GATE_TAU = 16.0
CHUNK = 64
GLA_IN = 2 * GLA_HEADS * GLA_DK + 2 * GLA_HEADS * GLA_DV + 2 * GATE_RANK
D_FF = ((8 * D_MODEL // 3 + 255) // 256) * 256
EPS = 1e-6

kernel_name = "hybrid_conv_gla_encoder"


def rms_norm(x, g):
    xf = x.astype(jnp.float32)
    y = xf * lax.rsqrt(jnp.mean(xf * xf, axis=-1, keepdims=True) + EPS)
    return (y * g.astype(jnp.float32)).astype(x.dtype)


def layer_norm(x, g, b):
    xf = x.astype(jnp.float32)
    mu = jnp.mean(xf, axis=-1, keepdims=True)
    xc = xf - mu
    y = xc * lax.rsqrt(jnp.mean(xc * xc, axis=-1, keepdims=True) + EPS)
    return (y * g.astype(jnp.float32) + b.astype(jnp.float32)).astype(x.dtype)


def depthwise_conv(x, w):
    k = w.shape[0]
    pad = (k - 1) // 2
    return lax.conv_general_dilated(
        x, w[:, None, :].astype(x.dtype), window_strides=(1,), padding=[(pad, pad)],
        dimension_numbers=("NWC", "WIO", "NWC"), feature_group_count=x.shape[-1])


def conv_hybrid_mixer(h, w_in, dw_w, dw_b, ln_g, ln_b, sc_w, w_out):
    u = h @ w_in
    a_val, a_gate, b_gate, c_gate, v = jnp.split(
        u, [D_A, 2 * D_A, 2 * D_A + D_B, 2 * D_A + 2 * D_B], axis=-1)
    a = a_val * jax.nn.sigmoid(a_gate)
    a = depthwise_conv(a, dw_w) + dw_b
    a = jax.nn.silu(layer_norm(a, ln_g, ln_b))
    bb = b_gate * depthwise_conv(c_gate * v, sc_w)
    return jnp.concatenate([a, bb], axis=-1) @ w_out


def gla_one_direction(q, k, v, log_a):
    b_, s_, h_, dk = q.shape
    dv = v.shape[-1]
    n = s_ // CHUNK
    qf = q.astype(jnp.float32).reshape(b_, n, CHUNK, h_, dk) * (dk ** -0.5)
    kf = k.astype(jnp.float32).reshape(b_, n, CHUNK, h_, dk)
    vf = v.astype(jnp.float32).reshape(b_, n, CHUNK, h_, dv)
    cum = jnp.cumsum(log_a.reshape(b_, n, CHUNK, h_, dk), axis=2)
    cum_last = cum[:, :, -1:]
    q_t = qf * jnp.exp(cum)
    k_t = kf * jnp.exp(-cum)
    k_end = kf * jnp.exp(cum_last - cum)
    scores = jnp.einsum("bnihk,bnjhk->bnhij", q_t, k_t)
    mask = jnp.tril(jnp.ones((CHUNK, CHUNK), dtype=bool))
    scores = jnp.where(mask, scores, 0.0)
    o_intra = jnp.einsum("bnhij,bnjhv->bnihv", scores, vf)
    kv = jnp.einsum("bnjhk,bnjhv->nbhkv", k_end, vf)
    decay = jnp.transpose(jnp.exp(cum_last[:, :, 0]), (1, 0, 2, 3))

    def step(state, inp):
        d, kv_n = inp
        return state * d[..., None] + kv_n, state

    _, states = lax.scan(step, jnp.zeros((b_, h_, dk, dv), jnp.float32), (decay, kv))
    o_inter = jnp.einsum("bnihk,nbhkv->bnihv", q_t, states)
    return (o_intra + o_inter).reshape(b_, s_, h_, dv)


def gla_mixer(h, w_in, wa2_f, ba2_f, wa2_b, ba2_b, gn_g, w_out):
    b_, s_, _ = h.shape
    u = h @ w_in
    qd = GLA_HEADS * GLA_DK
    vd = GLA_HEADS * GLA_DV
    q, k, v, r, g_f, g_b = jnp.split(
        u, [qd, 2 * qd, 2 * qd + vd, 2 * qd + 2 * vd, 2 * qd + 2 * vd + GATE_RANK], axis=-1)
    q = q.reshape(b_, s_, GLA_HEADS, GLA_DK)
    k = k.reshape(b_, s_, GLA_HEADS, GLA_DK)
    v = v.reshape(b_, s_, GLA_HEADS, GLA_DV)
    la_f = (jax.nn.log_sigmoid((g_f @ wa2_f + ba2_f).astype(jnp.float32)) / GATE_TAU)
    la_b = (jax.nn.log_sigmoid((g_b @ wa2_b + ba2_b).astype(jnp.float32)) / GATE_TAU)
    la_f = la_f.reshape(b_, s_, GLA_HEADS, GLA_DK)
    la_b = la_b.reshape(b_, s_, GLA_HEADS, GLA_DK)
    o_fwd = gla_one_direction(q, k, v, la_f)
    o_bwd = jnp.flip(gla_one_direction(jnp.flip(q, 1), jnp.flip(k, 1), jnp.flip(v, 1),
                                       jnp.flip(la_b, 1)), 1)
    o = o_fwd + o_bwd
    o = o * lax.rsqrt(jnp.mean(o * o, axis=-1, keepdims=True) + EPS)
    o = o * gn_g.astype(jnp.float32).reshape(GLA_HEADS, GLA_DV)
    o = o.reshape(b_, s_, GLA_HEADS * GLA_DV) * jax.nn.silu(r.astype(jnp.float32))
    return o.astype(h.dtype) @ w_out


def swiglu_ffn(h, w_gu, w_down):
    gu = h @ w_gu
    g, u = jnp.split(gu, 2, axis=-1)
    return (jax.nn.silu(g) * u) @ w_down


def setup_inputs(seed: int = 0) -> dict:
    key = jax.random.key(seed)
    ks = jax.random.split(key, 24)
    f32 = jnp.float32

    def nrm(k, shape, scale):
        return jax.random.normal(k, shape, f32) * scale

    def gain(k, shape):
        return 1.0 + 0.02 * jax.random.normal(k, shape, f32)

    return {
        "x": jax.random.normal(ks[0], (BATCH, SEQ, D_MODEL), f32),
        "mix_pre_g": gain(ks[1], (DEPTH, D_MODEL)),
        "mix_post_g": gain(ks[2], (DEPTH, D_MODEL)),
        "ffn_pre_g": gain(ks[3], (DEPTH, D_MODEL)),
        "ffn_post_g": gain(ks[4], (DEPTH, D_MODEL)),
        "cv_w_in": nrm(ks[5], (N_EVEN, D_MODEL, CONV_IN), D_MODEL ** -0.5),
        "cv_dw_w": nrm(ks[6], (N_EVEN, CONV_A_WIDTH, D_A), CONV_A_WIDTH ** -0.5),
        "cv_dw_b": nrm(ks[7], (N_EVEN, D_A), 0.02),
        "cv_ln_g": gain(ks[8], (N_EVEN, D_A)),
        "cv_ln_b": nrm(ks[9], (N_EVEN, D_A), 0.02),
        "cv_sc_w": nrm(ks[10], (N_EVEN, CONV_B_WIDTH, D_B), CONV_B_WIDTH ** -0.5),
        "cv_w_out": nrm(ks[11], (N_EVEN, D_A + D_B, D_MODEL), (D_A + D_B) ** -0.5),
        "gla_w_in": nrm(ks[12], (N_ODD, D_MODEL, GLA_IN), D_MODEL ** -0.5),
        "gla_wa2_f": nrm(ks[13], (N_ODD, GATE_RANK, GLA_HEADS * GLA_DK), GATE_RANK ** -0.5),
        "gla_ba2_f": nrm(ks[14], (N_ODD, GLA_HEADS * GLA_DK), 0.1),
        "gla_wa2_b": nrm(ks[15], (N_ODD, GATE_RANK, GLA_HEADS * GLA_DK), GATE_RANK ** -0.5),
        "gla_ba2_b": nrm(ks[16], (N_ODD, GLA_HEADS * GLA_DK), 0.1),
        "gla_gn_g": gain(ks[17], (N_ODD, GLA_HEADS * GLA_DV)),
        "gla_w_out": nrm(ks[18], (N_ODD, GLA_HEADS * GLA_DV, D_MODEL), (GLA_HEADS * GLA_DV) ** -0.5),
        "ffn_w_gu": nrm(ks[19], (DEPTH, D_MODEL, 2 * D_FF), D_MODEL ** -0.5),
        "ffn_w_down": nrm(ks[20], (DEPTH, D_FF, D_MODEL), D_FF ** -0.5),
    }


def reference(x, mix_pre_g, mix_post_g, ffn_pre_g, ffn_post_g,
              cv_w_in, cv_dw_w, cv_dw_b, cv_ln_g, cv_ln_b, cv_sc_w, cv_w_out,
              gla_w_in, gla_wa2_f, gla_ba2_f, gla_wa2_b, gla_ba2_b, gla_gn_g, gla_w_out,
              ffn_w_gu, ffn_w_down):
    for layer in range(DEPTH):
        h = rms_norm(x, mix_pre_g[layer])
        if layer % 2 == 0:
            i = layer // 2
            m = conv_hybrid_mixer(h, cv_w_in[i], cv_dw_w[i], cv_dw_b[i], cv_ln_g[i],
                                  cv_ln_b[i], cv_sc_w[i], cv_w_out[i])
        else:
            i = layer // 2
            m = gla_mixer(h, gla_w_in[i], gla_wa2_f[i], gla_ba2_f[i], gla_wa2_b[i],
                          gla_ba2_b[i], gla_gn_g[i], gla_w_out[i])
        x = x + rms_norm(m, mix_post_g[layer])
        h = rms_norm(x, ffn_pre_g[layer])
        f = swiglu_ffn(h, ffn_w_gu[layer], ffn_w_down[layer])
        x = x + rms_norm(f, ffn_post_g[layer])
    return x
```

```python
import functools

import jax
import jax.numpy as jnp
from jax import lax
from jax.experimental import pallas as pl
from jax.experimental.pallas import tpu as pltpu

D_MODEL = 1024
D_A = 512
D_B = 512
CONV_A_WIDTH = 31
CONV_B_WIDTH = 3
GLA_HEADS = 4
GLA_DK = 128
GLA_DV = 256
GATE_RANK = 16
GATE_TAU = 16.0
CHUNK = 64
D_FF = 2816
EPS = 1e-6

HALO_A = 16
HALO_B = 8
GATE_PAD = 128

TM = 512
TS = 512
VMEM_LIMIT = 56 * 1024 * 1024

F32 = jnp.float32
BF16 = jnp.bfloat16


def _rms(x, g):
    return x * lax.rsqrt(jnp.mean(x * x, axis=-1, keepdims=True) + EPS) * g


def _dot(a, b):
    return jnp.dot(a, b, preferred_element_type=F32)


def _const_spec(shape):
    zeros = (0,) * len(shape)
    return pl.BlockSpec(shape, lambda *_: zeros, pipeline_mode=pl.Buffered(1))


def _params(n_grid):
    return pltpu.CompilerParams(dimension_semantics=("arbitrary",) * n_grid,
                                vmem_limit_bytes=VMEM_LIMIT)


FF_CHUNK = 256


def _ffn_kernel(x_ref, pre_ref, wg_ref, wu_ref, wd_ref, post_ref, o_ref, act_ref):
    x = x_ref[...]
    h = _rms(x, pre_ref[...]).astype(BF16)
    for c in range(D_FF // FF_CHUNK):
        cols = slice(c * FF_CHUNK, (c + 1) * FF_CHUNK)
        g = _dot(h, wg_ref[:, cols])
        u = _dot(h, wu_ref[:, cols])
        act_ref[:, cols] = (g * jax.nn.sigmoid(g) * u).astype(BF16)
    f = _dot(act_ref[...], wd_ref[...])
    o_ref[...] = x + _rms(f, post_ref[...])


def _ffn(x, pre_g, w_g, w_u, w_d, post_g):
    t = x.shape[0]
    row = pl.BlockSpec((TM, D_MODEL), lambda i: (i, 0))
    return pl.pallas_call(
        _ffn_kernel,
        grid=(t // TM,),
        in_specs=[row, _const_spec((1, D_MODEL)), _const_spec((D_MODEL, D_FF)),
                  _const_spec((D_MODEL, D_FF)), _const_spec((D_FF, D_MODEL)),
                  _const_spec((1, D_MODEL))],
        out_specs=row,
        out_shape=jax.ShapeDtypeStruct((t, D_MODEL), F32),
        scratch_shapes=[pltpu.VMEM((TM, D_FF), BF16)],
        compiler_params=_params(1),
        name="ffn",
    )(x, pre_g, w_g, w_u, w_d, post_g)


def _conv_in_kernel(x_ref, pre_ref, w_ref, a_ref, cv_ref, bg_ref):
    h = _rms(x_ref[...], pre_ref[...]).astype(BF16)

    def proj(j):
        return _dot(h, w_ref[:, j * D_A:(j + 1) * D_A])

    a_ref[...] = proj(0) * jax.nn.sigmoid(proj(1))
    bg_ref[...] = proj(2)
    cv_ref[...] = proj(3) * proj(4)


def _conv_in(x, pre_g, w_in):
    t = x.shape[0]
    row = pl.BlockSpec((TM, D_MODEL), lambda i: (i, 0))
    half = pl.BlockSpec((TM, D_A), lambda i: (i, 0))
    out = jax.ShapeDtypeStruct((t, D_A), F32)
    return pl.pallas_call(
        _conv_in_kernel,
        grid=(t // TM,),
        in_specs=[row, _const_spec((1, D_MODEL)), _const_spec((D_MODEL, 5 * D_A))],
        out_specs=[half, half, half],
        out_shape=[out, out, out],
        compiler_params=_params(1),
        name="conv_in",
    )(x, pre_g, w_in)


CONV_ROWS = 32


def _conv_mix_kernel(x_ref, a_ref, ap_ref, an_ref, cv_ref, cp_ref, cn_ref, bg_ref,
                     dww_ref, dwb_ref, lng_ref, lnb_ref, scw_ref, wo_ref, post_ref,
                     o_ref, abuf, cbuf, mbuf):
    j = pl.program_id(1)
    first = j == 0
    last = j == pl.num_programs(1) - 1
    abuf[0:HALO_A, :] = jnp.where(first, 0.0, ap_ref[0])
    abuf[HALO_A:HALO_A + TS, :] = a_ref[0]
    abuf[HALO_A + TS:, :] = jnp.where(last, 0.0, an_ref[0])
    cbuf[0:HALO_B, :] = jnp.where(first, 0.0, cp_ref[0])
    cbuf[HALO_B:HALO_B + TS, :] = cv_ref[0]
    cbuf[HALO_B + TS:, :] = jnp.where(last, 0.0, cn_ref[0])

    pad_a = (CONV_A_WIDTH - 1) // 2
    pad_b = (CONV_B_WIDTH - 1) // 2
    for r in range(TS // CONV_ROWS):
        r0 = r * CONV_ROWS
        acc = jnp.zeros((CONV_ROWS, D_A), F32) + dwb_ref[...]
        for k in range(CONV_A_WIDTH):
            off = HALO_A + r0 + k - pad_a
            acc = acc + dww_ref[k:k + 1, :] * abuf[off:off + CONV_ROWS, :]
        mu = jnp.mean(acc, axis=-1, keepdims=True)
        xc = acc - mu
        y = xc * lax.rsqrt(jnp.mean(xc * xc, axis=-1, keepdims=True) + EPS)
        y = y * lng_ref[...] + lnb_ref[...]
        mbuf[r0:r0 + CONV_ROWS, 0:D_A] = (y * jax.nn.sigmoid(y)).astype(BF16)

        accb = jnp.zeros((CONV_ROWS, D_B), F32)
        for k in range(CONV_B_WIDTH):
            off = HALO_B + r0 + k - pad_b
            accb = accb + scw_ref[k:k + 1, :] * cbuf[off:off + CONV_ROWS, :]
        mbuf[r0:r0 + CONV_ROWS, D_A:] = (bg_ref[0, r0:r0 + CONV_ROWS, :] * accb).astype(BF16)

    m = _dot(mbuf[...], wo_ref[...])
    o_ref[0] = x_ref[0] + _rms(m, post_ref[...])


def _conv_mix(x, a, cv, bg, dw_w, dw_b, ln_g, ln_b, sc_w, w_out, post_g):
    b, s, _ = x.shape
    n_s = s // TS
    grid = (b, n_s)

    def main(width):
        return pl.BlockSpec((1, TS, width), lambda i, j: (i, j, 0))

    def prev(halo, width):
        per = TS // halo
        return pl.BlockSpec((1, halo, width), lambda i, j: (i, jnp.maximum(j * per - 1, 0), 0))

    def nxt(halo, width):
        per = TS // halo
        n_blocks = s // halo
        return pl.BlockSpec((1, halo, width),
                            lambda i, j: (i, jnp.minimum((j + 1) * per, n_blocks - 1), 0))

    return pl.pallas_call(
        _conv_mix_kernel,
        grid=grid,
        in_specs=[main(D_MODEL),
                  main(D_A), prev(HALO_A, D_A), nxt(HALO_A, D_A),
                  main(D_B), prev(HALO_B, D_B), nxt(HALO_B, D_B),
                  main(D_B),
                  _const_spec((CONV_A_WIDTH, D_A)), _const_spec((1, D_A)), _const_spec((1, D_A)),
                  _const_spec((1, D_A)), _const_spec((CONV_B_WIDTH, D_B)),
                  _const_spec((D_A + D_B, D_MODEL)), _const_spec((1, D_MODEL))],
        out_specs=main(D_MODEL),
        out_shape=jax.ShapeDtypeStruct((b, s, D_MODEL), F32),
        scratch_shapes=[pltpu.VMEM((TS + 2 * HALO_A, D_A), F32),
                        pltpu.VMEM((TS + 2 * HALO_B, D_B), F32),
                        pltpu.VMEM((TS, D_A + D_B), BF16)],
        compiler_params=_params(2),
        name="conv_mix",
    )(x, a, a, a, cv, cv, cv, bg, dw_w, dw_b, ln_g, ln_b, sc_w, w_out, post_g)


QD = GLA_HEADS * GLA_DK
VD = GLA_HEADS * GLA_DV


def _gla_in_kernel(x_ref, pre_ref, w_ref, wgate_ref, wa2_ref, ba2_ref,
                   q_ref, k_ref, v_ref, r_ref, la_ref):
    h = _rms(x_ref[...], pre_ref[...]).astype(BF16)
    q_ref[...] = _dot(h, w_ref[:, 0:QD])
    k_ref[...] = _dot(h, w_ref[:, QD:2 * QD])
    v_ref[...] = _dot(h, w_ref[:, 2 * QD:2 * QD + VD])
    r_ref[...] = _dot(h, w_ref[:, 2 * QD + VD:2 * QD + 2 * VD])
    gate = _dot(h, wgate_ref[...]).astype(BF16)
    z = _dot(gate, wa2_ref[...]) + ba2_ref[...]
    log_sig = jnp.minimum(z, 0.0) - jnp.log1p(jnp.exp(-jnp.abs(z)))
    la_ref[...] = log_sig / GATE_TAU


def _gla_in(x, pre_g, w_main, w_gate, wa2, ba2):
    t = x.shape[0]

    def rows(width):
        return pl.BlockSpec((TM, width), lambda i: (i, 0))

    def out(width):
        return jax.ShapeDtypeStruct((t, width), F32)

    return pl.pallas_call(
        _gla_in_kernel,
        grid=(t // TM,),
        in_specs=[rows(D_MODEL), _const_spec((1, D_MODEL)),
                  _const_spec((D_MODEL, 2 * QD + 2 * VD)), _const_spec((D_MODEL, GATE_PAD)),
                  _const_spec((GATE_PAD, 2 * QD)), _const_spec((1, 2 * QD))],
        out_specs=[rows(QD), rows(QD), rows(VD), rows(VD), rows(2 * QD)],
        out_shape=[out(QD), out(QD), out(VD), out(VD), out(2 * QD)],
        compiler_params=_params(1),
        name="gla_in",
    )(x, pre_g, w_main, w_gate, wa2, ba2)


def _gla_kernel(q_ref, k_ref, v_ref, laf_ref, lab_ref, o_ref, sb_ref, sf_ref):
    n_chunks = q_ref.shape[1] // CHUNK
    row = lax.broadcasted_iota(jnp.int32, (CHUNK, CHUNK), 0)
    col = lax.broadcasted_iota(jnp.int32, (CHUNK, CHUNK), 1)
    lower = row >= col
    tri_lo = lower.astype(F32)
    tri_up = (row <= col).astype(F32)
    scale = GLA_DK ** -0.5

    def cumsum(tri, la):
        return jnp.dot(tri, la, preferred_element_type=F32, precision=lax.Precision.HIGHEST)

    def kv_update(state_ref, v, k_end, total):
        kv = lax.dot_general(v.astype(BF16), k_end.astype(BF16), (((0,), (0,)), ((), ())),
                             preferred_element_type=F32)
        state_ref[...] = state_ref[...] * jnp.exp(total) + kv

    sf_ref[...] = jnp.zeros_like(sf_ref)

    def bwd_step(i, carry):
        n = n_chunks - 1 - i
        rows = pl.ds(pl.multiple_of(n * CHUNK, CHUNK), CHUNK)
        sb_ref[n] = sf_ref[...]
        cum = cumsum(tri_up, lab_ref[0, rows, :])
        total = cum[0:1, :]
        kv_update(sf_ref, v_ref[0, rows, :], k_ref[0, rows, :] * jnp.exp(total - cum), total)
        return carry

    lax.fori_loop(0, n_chunks, bwd_step, 0)

    sf_ref[...] = jnp.zeros_like(sf_ref)

    def fwd_step(n, carry):
        rows = pl.ds(pl.multiple_of(n * CHUNK, CHUNK), CHUNK)
        q = q_ref[0, rows, :] * scale
        k = k_ref[0, rows, :]
        v = v_ref[0, rows, :]
        cum_f = cumsum(tri_lo, laf_ref[0, rows, :])
        cum_b = cumsum(tri_up, lab_ref[0, rows, :])
        total_f = cum_f[CHUNK - 1:CHUNK, :]
        qf = (q * jnp.exp(cum_f)).astype(BF16)
        kf = (k * jnp.exp(-cum_f)).astype(BF16)
        qb = (q * jnp.exp(cum_b)).astype(BF16)
        kb = (k * jnp.exp(-cum_b)).astype(BF16)
        nt = (((1,), (1,)), ((), ()))
        s_f = lax.dot_general(qf, kf, nt, preferred_element_type=F32)
        s_b = lax.dot_general(qb, kb, nt, preferred_element_type=F32)
        scores = jnp.where(lower, s_f, 0.0) + jnp.where(row <= col, s_b, 0.0)
        o = _dot(scores.astype(BF16), v.astype(BF16))
        o = o + lax.dot_general(qf, sf_ref[...].astype(BF16), nt, preferred_element_type=F32)
        o = o + lax.dot_general(qb, sb_ref[n].astype(BF16), nt, preferred_element_type=F32)
        o_ref[0, rows, :] = o
        kv_update(sf_ref, v, k * jnp.exp(total_f - cum_f), total_f)
        return carry

    lax.fori_loop(0, n_chunks, fwd_step, 0)


def _gla(q, k, v, la):
    b, s, _ = q.shape
    n_chunks = s // CHUNK

    def head(width, first_block=0):
        return pl.BlockSpec((1, s, width), lambda i, h: (i, 0, first_block + h))

    return pl.pallas_call(
        _gla_kernel,
        grid=(b, GLA_HEADS),
        in_specs=[head(GLA_DK), head(GLA_DK), head(GLA_DV), head(GLA_DK),
                  head(GLA_DK, GLA_HEADS)],
        out_specs=head(GLA_DV),
        out_shape=jax.ShapeDtypeStruct((b, s, VD), F32),
        scratch_shapes=[pltpu.VMEM((n_chunks, GLA_DV, GLA_DK), F32),
                        pltpu.VMEM((GLA_DV, GLA_DK), F32)],
        compiler_params=_params(2),
        name="gla",
    )(q, k, v, la, la)


def _gla_out_kernel(x_ref, o_ref, r_ref, gn_ref, wo_ref, post_ref, y_ref, mbuf):
    for hd in range(GLA_HEADS):
        cols = slice(hd * GLA_DV, (hd + 1) * GLA_DV)
        o = o_ref[:, cols]
        o = o * lax.rsqrt(jnp.mean(o * o, axis=-1, keepdims=True) + EPS) * gn_ref[:, cols]
        r = r_ref[:, cols]
        mbuf[:, cols] = (o * (r * jax.nn.sigmoid(r))).astype(BF16)
    m = _dot(mbuf[...], wo_ref[...])
    y_ref[...] = x_ref[...] + _rms(m, post_ref[...])


def _gla_out(x, o, r, gn_g, w_out, post_g):
    t = x.shape[0]
    row = pl.BlockSpec((TM, D_MODEL), lambda i: (i, 0))
    return pl.pallas_call(
        _gla_out_kernel,
        grid=(t // TM,),
        in_specs=[row, row, row, _const_spec((1, VD)), _const_spec((VD, D_MODEL)),
                  _const_spec((1, D_MODEL))],
        out_specs=row,
        out_shape=jax.ShapeDtypeStruct((t, D_MODEL), F32),
        scratch_shapes=[pltpu.VMEM((TM, VD), BF16)],
        compiler_params=_params(1),
        name="gla_out",
    )(x, o, r, gn_g, w_out, post_g)


def kernel(x, mix_pre_g, mix_post_g, ffn_pre_g, ffn_post_g, cv_w_in, cv_dw_w, cv_dw_b, cv_ln_g,
           cv_ln_b, cv_sc_w, cv_w_out, gla_w_in, gla_wa2_f, gla_ba2_f, gla_wa2_b, gla_ba2_b,
           gla_gn_g, gla_w_out, ffn_w_gu, ffn_w_down):
    b, s, d = x.shape
    t = b * s

    def vec(g):
        return g.reshape(1, -1)

    def ffn(xt, layer):
        w_gu = ffn_w_gu[layer].astype(BF16)
        return _ffn(xt, vec(ffn_pre_g[layer]), w_gu[:, :D_FF], w_gu[:, D_FF:],
                    ffn_w_down[layer].astype(BF16), vec(ffn_post_g[layer]))

    xt = x.reshape(t, d)

    a, cv, bg = _conv_in(xt, vec(mix_pre_g[0]), cv_w_in[0].astype(BF16))
    xs = _conv_mix(x, a.reshape(b, s, D_A), cv.reshape(b, s, D_B), bg.reshape(b, s, D_B),
                   cv_dw_w[0], vec(cv_dw_b[0]), vec(cv_ln_g[0]), vec(cv_ln_b[0]), cv_sc_w[0],
                   cv_w_out[0].astype(BF16), vec(mix_post_g[0]))
    xt = ffn(xs.reshape(t, d), 0)

    w_in = gla_w_in[0]
    n_main = 2 * QD + 2 * VD
    w_gate = jnp.pad(w_in[:, n_main:], ((0, 0), (0, GATE_PAD - 2 * GATE_RANK))).astype(BF16)
    wa2 = jnp.zeros((GATE_PAD, 2 * QD), F32)
    wa2 = wa2.at[:GATE_RANK, :QD].set(gla_wa2_f[0])
    wa2 = wa2.at[GATE_RANK:2 * GATE_RANK, QD:].set(gla_wa2_b[0])
    ba2 = jnp.concatenate([gla_ba2_f[0], gla_ba2_b[0]]).reshape(1, 2 * QD)
    q, k, v, r, la = _gla_in(xt, vec(mix_pre_g[1]), w_in[:, :n_main].astype(BF16), w_gate,
                             wa2.astype(BF16), ba2)
    o = _gla(q.reshape(b, s, QD), k.reshape(b, s, QD), v.reshape(b, s, VD),
             la.reshape(b, s, 2 * QD))
    xt = _gla_out(xt, o.reshape(t, VD), r, vec(gla_gn_g[0]), gla_w_out[0].astype(BF16),
                  vec(mix_post_g[1]))
    xt = ffn(xt, 1)
    return xt.reshape(b, s, d)
```

```python
import functools

import jax
import jax.numpy as jnp
from jax import lax
from jax.experimental import pallas as pl
from jax.experimental.pallas import tpu as pltpu

D_MODEL = 1024
D_A = 512
D_B = 512
CONV_A_WIDTH = 31
CONV_B_WIDTH = 3
GLA_HEADS = 4
GLA_DK = 128
GLA_DV = 256
GATE_RANK = 16
GATE_TAU = 16.0
CHUNK = 64
D_FF = 2816
EPS = 1e-6

HALO_A = 16
HALO_B = 8
GATE_PAD = 128

TM = 512
TS = 512
VMEM_LIMIT = 56 * 1024 * 1024

F32 = jnp.float32
BF16 = jnp.bfloat16


def _rms(x, g):
    return x * lax.rsqrt(jnp.mean(x * x, axis=-1, keepdims=True) + EPS) * g


def _dot(a, b):
    return jnp.dot(a, b, preferred_element_type=F32)


def _const_spec(shape):
    zeros = (0,) * len(shape)
    return pl.BlockSpec(shape, lambda *_: zeros, pipeline_mode=pl.Buffered(1))


def _params(n_grid):
    return pltpu.CompilerParams(dimension_semantics=("arbitrary",) * n_grid,
                                vmem_limit_bytes=VMEM_LIMIT)


FF_CHUNK = 256


def _ffn_kernel(x_ref, pre_ref, wg_ref, wu_ref, wd_ref, post_ref, o_ref, act_ref):
    x = x_ref[...]
    h = _rms(x, pre_ref[...]).astype(BF16)
    for c in range(D_FF // FF_CHUNK):
        cols = slice(c * FF_CHUNK, (c + 1) * FF_CHUNK)
        g = _dot(h, wg_ref[:, cols])
        u = _dot(h, wu_ref[:, cols])
        act_ref[:, cols] = (g * jax.nn.sigmoid(g) * u).astype(BF16)
    f = _dot(act_ref[...], wd_ref[...])
    o_ref[...] = x + _rms(f, post_ref[...])


def _ffn(x, pre_g, w_g, w_u, w_d, post_g):
    t = x.shape[0]
    row = pl.BlockSpec((TM, D_MODEL), lambda i: (i, 0))
    return pl.pallas_call(
        _ffn_kernel,
        grid=(t // TM,),
        in_specs=[row, _const_spec((1, D_MODEL)), _const_spec((D_MODEL, D_FF)),
                  _const_spec((D_MODEL, D_FF)), _const_spec((D_FF, D_MODEL)),
                  _const_spec((1, D_MODEL))],
        out_specs=row,
        out_shape=jax.ShapeDtypeStruct((t, D_MODEL), F32),
        scratch_shapes=[pltpu.VMEM((TM, D_FF), BF16)],
        compiler_params=_params(1),
        name="ffn",
    )(x, pre_g, w_g, w_u, w_d, post_g)


def _conv_in_kernel(x_ref, pre_ref, w_ref, a_ref, cv_ref, bg_ref):
    h = _rms(x_ref[...], pre_ref[...]).astype(BF16)

    def proj(j):
        return _dot(h, w_ref[:, j * D_A:(j + 1) * D_A])

    a_ref[...] = proj(0) * jax.nn.sigmoid(proj(1))
    bg_ref[...] = proj(2)
    cv_ref[...] = proj(3) * proj(4)


def _conv_in(x, pre_g, w_in):
    t = x.shape[0]
    row = pl.BlockSpec((TM, D_MODEL), lambda i: (i, 0))
    half = pl.BlockSpec((TM, D_A), lambda i: (i, 0))
    out = jax.ShapeDtypeStruct((t, D_A), F32)
    return pl.pallas_call(
        _conv_in_kernel,
        grid=(t // TM,),
        in_specs=[row, _const_spec((1, D_MODEL)), _const_spec((D_MODEL, 5 * D_A))],
        out_specs=[half, half, half],
        out_shape=[out, out, out],
        compiler_params=_params(1),
        name="conv_in",
    )(x, pre_g, w_in)


CONV_ROWS = 32


def _conv_mix_kernel(x_ref, a_ref, ap_ref, an_ref, cv_ref, cp_ref, cn_ref, bg_ref,
                     dww_ref, dwb_ref, lng_ref, lnb_ref, scw_ref, wo_ref, post_ref,
                     o_ref, abuf, cbuf, mbuf):
    j = pl.program_id(1)
    first = j == 0
    last = j == pl.num_programs(1) - 1
    abuf[0:HALO_A, :] = jnp.where(first, 0.0, ap_ref[0])
    abuf[HALO_A:HALO_A + TS, :] = a_ref[0]
    abuf[HALO_A + TS:, :] = jnp.where(last, 0.0, an_ref[0])
    cbuf[0:HALO_B, :] = jnp.where(first, 0.0, cp_ref[0])
    cbuf[HALO_B:HALO_B + TS, :] = cv_ref[0]
    cbuf[HALO_B + TS:, :] = jnp.where(last, 0.0, cn_ref[0])

    pad_a = (CONV_A_WIDTH - 1) // 2
    pad_b = (CONV_B_WIDTH - 1) // 2
    for r in range(TS // CONV_ROWS):
        r0 = r * CONV_ROWS
        acc = jnp.zeros((CONV_ROWS, D_A), F32) + dwb_ref[...]
        for k in range(CONV_A_WIDTH):
            off = HALO_A + r0 + k - pad_a
            acc = acc + dww_ref[k:k + 1, :] * abuf[off:off + CONV_ROWS, :]
        mu = jnp.mean(acc, axis=-1, keepdims=True)
        xc = acc - mu
        y = xc * lax.rsqrt(jnp.mean(xc * xc, axis=-1, keepdims=True) + EPS)
        y = y * lng_ref[...] + lnb_ref[...]
        mbuf[r0:r0 + CONV_ROWS, 0:D_A] = (y * jax.nn.sigmoid(y)).astype(BF16)

        accb = jnp.zeros((CONV_ROWS, D_B), F32)
        for k in range(CONV_B_WIDTH):
            off = HALO_B + r0 + k - pad_b
            accb = accb + scw_ref[k:k + 1, :] * cbuf[off:off + CONV_ROWS, :]
        mbuf[r0:r0 + CONV_ROWS, D_A:] = (bg_ref[0, r0:r0 + CONV_ROWS, :] * accb).astype(BF16)

    m = _dot(mbuf[...], wo_ref[...])
    o_ref[0] = x_ref[0] + _rms(m, post_ref[...])


def _conv_mix(x, a, cv, bg, dw_w, dw_b, ln_g, ln_b, sc_w, w_out, post_g):
    b, s, _ = x.shape
    n_s = s // TS
    grid = (b, n_s)

    def main(width):
        return pl.BlockSpec((1, TS, width), lambda i, j: (i, j, 0))

    def prev(halo, width):
        per = TS // halo
        return pl.BlockSpec((1, halo, width), lambda i, j: (i, jnp.maximum(j * per - 1, 0), 0))

    def nxt(halo, width):
        per = TS // halo
        n_blocks = s // halo
        return pl.BlockSpec((1, halo, width),
                            lambda i, j: (i, jnp.minimum((j + 1) * per, n_blocks - 1), 0))

    return pl.pallas_call(
        _conv_mix_kernel,
        grid=grid,
        in_specs=[main(D_MODEL),
                  main(D_A), prev(HALO_A, D_A), nxt(HALO_A, D_A),
                  main(D_B), prev(HALO_B, D_B), nxt(HALO_B, D_B),
                  main(D_B),
                  _const_spec((CONV_A_WIDTH, D_A)), _const_spec((1, D_A)), _const_spec((1, D_A)),
                  _const_spec((1, D_A)), _const_spec((CONV_B_WIDTH, D_B)),
                  _const_spec((D_A + D_B, D_MODEL)), _const_spec((1, D_MODEL))],
        out_specs=main(D_MODEL),
        out_shape=jax.ShapeDtypeStruct((b, s, D_MODEL), F32),
        scratch_shapes=[pltpu.VMEM((TS + 2 * HALO_A, D_A), F32),
                        pltpu.VMEM((TS + 2 * HALO_B, D_B), F32),
                        pltpu.VMEM((TS, D_A + D_B), BF16)],
        compiler_params=_params(2),
        name="conv_mix",
    )(x, a, a, a, cv, cv, cv, bg, dw_w, dw_b, ln_g, ln_b, sc_w, w_out, post_g)


QD = GLA_HEADS * GLA_DK
VD = GLA_HEADS * GLA_DV


def _gla_in_kernel(x_ref, pre_ref, w_ref, wgate_ref, wa2_ref, ba2_ref,
                   q_ref, k_ref, v_ref, r_ref, la_ref):
    h = _rms(x_ref[...], pre_ref[...]).astype(BF16)
    q_ref[...] = _dot(h, w_ref[:, 0:QD])
    k_ref[...] = _dot(h, w_ref[:, QD:2 * QD])
    v_ref[...] = _dot(h, w_ref[:, 2 * QD:2 * QD + VD])
    r_ref[...] = _dot(h, w_ref[:, 2 * QD + VD:2 * QD + 2 * VD])
    gate = _dot(h, wgate_ref[...]).astype(BF16)
    z = _dot(gate, wa2_ref[...]) + ba2_ref[...]
    log_sig = jnp.minimum(z, 0.0) - jnp.log1p(jnp.exp(-jnp.abs(z)))
    la_ref[...] = log_sig / GATE_TAU


def _gla_in(x, pre_g, w_main, w_gate, wa2, ba2):
    t = x.shape[0]

    def rows(width):
        return pl.BlockSpec((TM, width), lambda i: (i, 0))

    def out(width):
        return jax.ShapeDtypeStruct((t, width), F32)

    return pl.pallas_call(
        _gla_in_kernel,
        grid=(t // TM,),
        in_specs=[rows(D_MODEL), _const_spec((1, D_MODEL)),
                  _const_spec((D_MODEL, 2 * QD + 2 * VD)), _const_spec((D_MODEL, GATE_PAD)),
                  _const_spec((GATE_PAD, 2 * QD)), _const_spec((1, 2 * QD))],
        out_specs=[rows(QD), rows(QD), rows(VD), rows(VD), rows(2 * QD)],
        out_shape=[out(QD), out(QD), out(VD), out(VD), out(2 * QD)],
        compiler_params=_params(1),
        name="gla_in",
    )(x, pre_g, w_main, w_gate, wa2, ba2)


GLA_GROUP = 4


def _gla_kernel(q_ref, k_ref, v_ref, laf_ref, lab_ref, o_ref,
                qfb_ref, kf_ref, kb_ref, ke_ref, dec_ref, kv_ref, st_ref, cur_ref):
    n_chunks = q_ref.shape[1] // CHUNK
    row = lax.broadcasted_iota(jnp.int32, (CHUNK, CHUNK), 0)
    col = lax.broadcasted_iota(jnp.int32, (CHUNK, CHUNK), 1)
    lower = row >= col
    upper = row <= col
    tri = lower.astype(BF16)
    tri3 = jnp.concatenate([tri, tri, tri], axis=1)
    scale = GLA_DK ** -0.5
    nt = (((1,), (1,)), ((), ()))
    tn = (((0,), (0,)), ((), ()))

    def chunk_rows(n):
        return pl.ds(pl.multiple_of(n * CHUNK, CHUNK), CHUNK)

    def group(i):
        return [i * GLA_GROUP + g for g in range(GLA_GROUP)]

    def prepare(i, carry):
        pres = []
        for n in group(i):
            rows = chunk_rows(n)
            la = jnp.concatenate([laf_ref[0, rows, :], lab_ref[0, rows, :]], axis=1)
            hi = la.astype(BF16)
            r1 = la - hi.astype(F32)
            mid = r1.astype(BF16)
            lo = (r1 - mid.astype(F32)).astype(BF16)
            pres.append(_dot(tri3, jnp.concatenate([hi, mid, lo], axis=0)))
        for n, pre in zip(group(i), pres):
            rows = chunk_rows(n)
            cum_f = pre[:, :GLA_DK]
            pre_b = pre[:, GLA_DK:]
            tot_f = cum_f[CHUNK - 1:CHUNK, :]
            tot_b = pre_b[CHUNK - 1:CHUNK, :]
            cum_b = tot_b - pre_b + lab_ref[0, rows, :]
            q = q_ref[0, rows, :] * scale
            k = k_ref[0, rows, :]
            kf = k * jnp.exp(-cum_f)
            kb = k * jnp.exp(-cum_b)
            dec = jnp.concatenate([jnp.exp(tot_f), jnp.exp(tot_b)], axis=1)
            qfb_ref[rows, :GLA_DK] = (q * jnp.exp(cum_f)).astype(BF16)
            qfb_ref[rows, GLA_DK:] = (q * jnp.exp(cum_b)).astype(BF16)
            kf_ref[rows, :] = kf.astype(BF16)
            kb_ref[rows, :] = kb.astype(BF16)
            ke_ref[rows, :GLA_DK] = (kf * dec[:, :GLA_DK]).astype(BF16)
            ke_ref[rows, GLA_DK:] = (kb * dec[:, GLA_DK:]).astype(BF16)
            dec_ref[n] = jnp.broadcast_to(dec, dec_ref.shape[1:])

    def intra(i, carry):
        chunks = group(i)
        all_rows = [chunk_rows(n) for n in chunks]
        s_f = [lax.dot_general(qfb_ref[rows, :GLA_DK], kf_ref[rows, :], nt,
                               preferred_element_type=F32) for rows in all_rows]
        s_b = [lax.dot_general(qfb_ref[rows, GLA_DK:], kb_ref[rows, :], nt,
                               preferred_element_type=F32) for rows in all_rows]
        for n, rows in zip(chunks, all_rows):
            kv_ref[n] = lax.dot_general(v_ref[0, rows, :].astype(BF16), ke_ref[rows, :], tn,
                                        preferred_element_type=F32)
        for rows, sf, sb in zip(all_rows, s_f, s_b):
            scores = jnp.where(lower, sf, 0.0) + jnp.where(upper, sb, 0.0)
            o_ref[0, rows, :] = _dot(scores.astype(BF16), v_ref[0, rows, :].astype(BF16))

    def scan(i, carry):
        n = i
        m = n_chunks - 1 - i
        st_ref[n, :, :GLA_DK] = cur_ref[:, :GLA_DK].astype(BF16)
        st_ref[m, :, GLA_DK:] = cur_ref[:, GLA_DK:].astype(BF16)
        cur_ref[:, :GLA_DK] = (cur_ref[:, :GLA_DK] * dec_ref[n][0:1, :GLA_DK]
                               + kv_ref[n][:, :GLA_DK])
        cur_ref[:, GLA_DK:] = (cur_ref[:, GLA_DK:] * dec_ref[m][0:1, GLA_DK:]
                               + kv_ref[m][:, GLA_DK:])

    def inter(i, carry):
        for n in group(i):
            rows = chunk_rows(n)
            o_ref[0, rows, :] += lax.dot_general(qfb_ref[rows, :], st_ref[n], nt,
                                                 preferred_element_type=F32)

    def run(body, trips):
        def step(i, carry):
            body(i, carry)
            return carry
        lax.fori_loop(0, trips, step, 0)

    n_groups = n_chunks // GLA_GROUP
    run(prepare, n_groups)
    run(intra, n_groups)
    cur_ref[...] = jnp.zeros_like(cur_ref)
    run(scan, n_chunks)
    run(inter, n_groups)


def _gla(q, k, v, la):
    b, s, _ = q.shape
    n_chunks = s // CHUNK

    def head(width, first_block=0):
        return pl.BlockSpec((1, s, width), lambda i, h: (i, 0, first_block + h))

    return pl.pallas_call(
        _gla_kernel,
        grid=(b, GLA_HEADS),
        in_specs=[head(GLA_DK), head(GLA_DK), head(GLA_DV), head(GLA_DK),
                  head(GLA_DK, GLA_HEADS)],
        out_specs=head(GLA_DV),
        out_shape=jax.ShapeDtypeStruct((b, s, VD), F32),
        scratch_shapes=[pltpu.VMEM((s, 2 * GLA_DK), BF16),
                        pltpu.VMEM((s, GLA_DK), BF16),
                        pltpu.VMEM((s, GLA_DK), BF16),
                        pltpu.VMEM((s, 2 * GLA_DK), BF16),
                        pltpu.VMEM((n_chunks, 8, 2 * GLA_DK), F32),
                        pltpu.VMEM((n_chunks, GLA_DV, 2 * GLA_DK), F32),
                        pltpu.VMEM((n_chunks, GLA_DV, 2 * GLA_DK), BF16),
                        pltpu.VMEM((GLA_DV, 2 * GLA_DK), F32)],
        compiler_params=_params(2),
        name="gla",
    )(q, k, v, la, la)


def _gla_out_kernel(x_ref, o_ref, r_ref, gn_ref, wo_ref, post_ref, y_ref, mbuf):
    for hd in range(GLA_HEADS):
        cols = slice(hd * GLA_DV, (hd + 1) * GLA_DV)
        o = o_ref[:, cols]
        o = o * lax.rsqrt(jnp.mean(o * o, axis=-1, keepdims=True) + EPS) * gn_ref[:, cols]
        r = r_ref[:, cols]
        mbuf[:, cols] = (o * (r * jax.nn.sigmoid(r))).astype(BF16)
    m = _dot(mbuf[...], wo_ref[...])
    y_ref[...] = x_ref[...] + _rms(m, post_ref[...])


def _gla_out(x, o, r, gn_g, w_out, post_g):
    t = x.shape[0]
    row = pl.BlockSpec((TM, D_MODEL), lambda i: (i, 0))
    return pl.pallas_call(
        _gla_out_kernel,
        grid=(t // TM,),
        in_specs=[row, row, row, _const_spec((1, VD)), _const_spec((VD, D_MODEL)),
                  _const_spec((1, D_MODEL))],
        out_specs=row,
        out_shape=jax.ShapeDtypeStruct((t, D_MODEL), F32),
        scratch_shapes=[pltpu.VMEM((TM, VD), BF16)],
        compiler_params=_params(1),
        name="gla_out",
    )(x, o, r, gn_g, w_out, post_g)


def kernel(x, mix_pre_g, mix_post_g, ffn_pre_g, ffn_post_g, cv_w_in, cv_dw_w, cv_dw_b, cv_ln_g,
           cv_ln_b, cv_sc_w, cv_w_out, gla_w_in, gla_wa2_f, gla_ba2_f, gla_wa2_b, gla_ba2_b,
           gla_gn_g, gla_w_out, ffn_w_gu, ffn_w_down):
    b, s, d = x.shape
    t = b * s

    def vec(g):
        return g.reshape(1, -1)

    def ffn(xt, layer):
        w_gu = ffn_w_gu[layer].astype(BF16)
        return _ffn(xt, vec(ffn_pre_g[layer]), w_gu[:, :D_FF], w_gu[:, D_FF:],
                    ffn_w_down[layer].astype(BF16), vec(ffn_post_g[layer]))

    xt = x.reshape(t, d)

    a, cv, bg = _conv_in(xt, vec(mix_pre_g[0]), cv_w_in[0].astype(BF16))
    xs = _conv_mix(x, a.reshape(b, s, D_A), cv.reshape(b, s, D_B), bg.reshape(b, s, D_B),
                   cv_dw_w[0], vec(cv_dw_b[0]), vec(cv_ln_g[0]), vec(cv_ln_b[0]), cv_sc_w[0],
                   cv_w_out[0].astype(BF16), vec(mix_post_g[0]))
    xt = ffn(xs.reshape(t, d), 0)

    w_in = gla_w_in[0]
    n_main = 2 * QD + 2 * VD
    w_gate = jnp.pad(w_in[:, n_main:], ((0, 0), (0, GATE_PAD - 2 * GATE_RANK))).astype(BF16)
    wa2 = jnp.zeros((GATE_PAD, 2 * QD), F32)
    wa2 = wa2.at[:GATE_RANK, :QD].set(gla_wa2_f[0])
    wa2 = wa2.at[GATE_RANK:2 * GATE_RANK, QD:].set(gla_wa2_b[0])
    ba2 = jnp.concatenate([gla_ba2_f[0], gla_ba2_b[0]]).reshape(1, 2 * QD)
    q, k, v, r, la = _gla_in(xt, vec(mix_pre_g[1]), w_in[:, :n_main].astype(BF16), w_gate,
                             wa2.astype(BF16), ba2)
    o = _gla(q.reshape(b, s, QD), k.reshape(b, s, QD), v.reshape(b, s, VD),
             la.reshape(b, s, 2 * QD))
    xt = _gla_out(xt, o.reshape(t, VD), r, vec(gla_gn_g[0]), gla_w_out[0].astype(BF16),
                  vec(mix_post_g[1]))
    xt = ffn(xt, 1)
    return xt.reshape(b, s, d)
```

```python
import jax
import jax.numpy as jnp
from jax import lax
from jax.experimental import pallas as pl
from jax.experimental.pallas import tpu as pltpu

D_MODEL = 1024
D_A = 512
D_B = 512
CONV_A_WIDTH = 31
CONV_B_WIDTH = 3
GLA_HEADS = 4
GLA_DK = 128
GLA_DV = 256
GATE_RANK = 16
GATE_TAU = 16.0
CHUNK = 64
D_FF = 2816
EPS = 1e-6

HALO_A = 16
HALO_B = 8
GATE_PAD = 128

TM = 512
TS = 512
VMEM_LIMIT = 56 * 1024 * 1024

F32 = jnp.float32
BF16 = jnp.bfloat16


def _rms(x, g):
    return x * lax.rsqrt(jnp.mean(x * x, axis=-1, keepdims=True) + EPS) * g


def _dot(a, b):
    return jnp.dot(a, b, preferred_element_type=F32)


def _const_spec(shape):
    zeros = (0,) * len(shape)
    return pl.BlockSpec(shape, lambda *_: zeros, pipeline_mode=pl.Buffered(1))


def _params(n_grid):
    return pltpu.CompilerParams(dimension_semantics=("arbitrary",) * n_grid,
                                vmem_limit_bytes=VMEM_LIMIT)


FF_CHUNK = 256


def _ffn_kernel(x_ref, pre_ref, wgu_ref, wd_ref, post_ref, o_ref, act_ref):
    x = x_ref[...]
    h = _rms(x, pre_ref[...]).astype(BF16)
    for c in range(D_FF // FF_CHUNK):
        lo = c * FF_CHUNK
        g = _dot(h, wgu_ref[:, lo:lo + FF_CHUNK])
        u = _dot(h, wgu_ref[:, D_FF + lo:D_FF + lo + FF_CHUNK])
        act_ref[:, lo:lo + FF_CHUNK] = (g * jax.nn.sigmoid(g) * u).astype(BF16)
    f = _dot(act_ref[...], wd_ref[...])
    o_ref[...] = x + _rms(f, post_ref[...])


def _ffn(x, pre_g, w_gu, w_d, post_g):
    t = x.shape[0]
    row = pl.BlockSpec((TM, D_MODEL), lambda i: (i, 0))
    return pl.pallas_call(
        _ffn_kernel,
        grid=(t // TM,),
        in_specs=[row, _const_spec((1, D_MODEL)), _const_spec((D_MODEL, 2 * D_FF)),
                  _const_spec((D_FF, D_MODEL)), _const_spec((1, D_MODEL))],
        out_specs=row,
        out_shape=jax.ShapeDtypeStruct((t, D_MODEL), F32),
        scratch_shapes=[pltpu.VMEM((TM, D_FF), BF16)],
        compiler_params=_params(1),
        name="ffn",
    )(x, pre_g, w_gu, w_d, post_g)


def _conv_in_kernel(x_ref, pre_ref, w_ref, a_ref, cv_ref, bg_ref):
    h = _rms(x_ref[...], pre_ref[...]).astype(BF16)

    def proj(j):
        return _dot(h, w_ref[:, j * D_A:(j + 1) * D_A])

    a_ref[...] = proj(0) * jax.nn.sigmoid(proj(1))
    bg_ref[...] = proj(2)
    cv_ref[...] = proj(3) * proj(4)


def _conv_in(x, pre_g, w_in):
    t = x.shape[0]
    row = pl.BlockSpec((TM, D_MODEL), lambda i: (i, 0))
    half = pl.BlockSpec((TM, D_A), lambda i: (i, 0))
    out = jax.ShapeDtypeStruct((t, D_A), F32)
    return pl.pallas_call(
        _conv_in_kernel,
        grid=(t // TM,),
        in_specs=[row, _const_spec((1, D_MODEL)), _const_spec((D_MODEL, 5 * D_A))],
        out_specs=[half, half, half],
        out_shape=[out, out, out],
        compiler_params=_params(1),
        name="conv_in",
    )(x, pre_g, w_in)


CONV_ROWS = 32
CONV_LANES = 256
NORM_ROWS = 32
NORM_UNROLL = 8
SUBLANES = 8
PAD_A = (CONV_A_WIDTH - 1) // 2
PAD_B = (CONV_B_WIDTH - 1) // 2
SHIFTS_B = sorted({(HALO_B + k - PAD_B) % SUBLANES for k in range(CONV_B_WIDTH)})


def _conv_mix_kernel(x_ref, a_ref, ap_ref, an_ref, cv_ref, cp_ref, cn_ref, bg_ref,
                     dww_ref, dwb_ref, lng_ref, lnb_ref, scw_ref, wo_ref, post_ref,
                     o_ref, ash, csh, conv_ref, mbuf):
    j = pl.program_id(1)
    first = j == 0
    last = j == pl.num_programs(1) - 1
    ash[0, 0:HALO_A, :] = jnp.where(first, 0.0, ap_ref[0])
    ash[0, HALO_A:HALO_A + TS, :] = a_ref[0]
    ash[0, HALO_A + TS:, :] = jnp.where(last, 0.0, an_ref[0])
    csh[0, 0:HALO_B, :] = jnp.where(first, 0.0, cp_ref[0])
    csh[0, HALO_B:HALO_B + TS, :] = cv_ref[0]
    csh[0, HALO_B + TS:, :] = jnp.where(last, 0.0, cn_ref[0])
    rows_a = TS + 2 * HALO_A - SUBLANES
    for r in range(1, SUBLANES):
        ash[r, 0:rows_a, :] = ash[0, r:r + rows_a, :]
    rows_b = TS + 2 * HALO_B - SUBLANES
    for i, r in enumerate(SHIFTS_B):
        if r:
            csh[i, 0:rows_b, :] = csh[0, r:r + rows_b, :]

    groups = CONV_ROWS // SUBLANES

    for cb in range(D_A // CONV_LANES):
        lanes = slice(cb * CONV_LANES, (cb + 1) * CONV_LANES)

        def conv_block(i, carry, lanes=lanes):
            r0 = pl.multiple_of(i * CONV_ROWS, CONV_ROWS)
            acc = None
            for r in range(SUBLANES):
                taps = [k for k in range(CONV_A_WIDTH) if (HALO_A + k - PAD_A) % SUBLANES == r]
                offs = [HALO_A + k - PAD_A - r for k in taps]
                lo = min(offs)
                window = ash[r, pl.ds(r0 + lo, max(offs) - lo + CONV_ROWS), lanes]
                for k, off in zip(taps, offs):
                    slab = window[off - lo:off - lo + CONV_ROWS]
                    term = slab.reshape(groups, SUBLANES, CONV_LANES) * dww_ref[k, :, lanes][None]
                    acc = term if acc is None else acc + term
            conv_ref[pl.ds(r0, CONV_ROWS), lanes] = acc.reshape(CONV_ROWS, CONV_LANES)
            return carry

        lax.fori_loop(0, TS // CONV_ROWS, conv_block, 0)

    def norm_block(r0):
        rows = pl.ds(r0, NORM_ROWS)
        acc = conv_ref[rows, :] + dwb_ref[...]
        mu = jnp.mean(acc, axis=-1, keepdims=True)
        xc = acc - mu
        y = xc * lax.rsqrt(jnp.mean(xc * xc, axis=-1, keepdims=True) + EPS)
        y = y * lng_ref[...] + lnb_ref[...]
        mbuf[rows, 0:D_A] = (y * jax.nn.sigmoid(y)).astype(BF16)

        accb = None
        for k in range(CONV_B_WIDTH):
            off = HALO_B + k - PAD_B
            slab = csh[SHIFTS_B.index(off % SUBLANES), pl.ds(r0 + off - off % SUBLANES, NORM_ROWS), :]
            term = slab.reshape(NORM_ROWS // SUBLANES, SUBLANES, D_B) * scw_ref[k][None]
            accb = term if accb is None else accb + term
        mbuf[rows, D_A:] = (bg_ref[0, rows, :] * accb.reshape(NORM_ROWS, D_B)).astype(BF16)

    def norm_step(i, carry):
        for u in range(NORM_UNROLL):
            norm_block(pl.multiple_of((i * NORM_UNROLL + u) * NORM_ROWS, NORM_ROWS))
        return carry

    lax.fori_loop(0, TS // (NORM_ROWS * NORM_UNROLL), norm_step, 0)

    m = _dot(mbuf[...], wo_ref[...])
    o_ref[0] = x_ref[0] + _rms(m, post_ref[...])


def _conv_mix(x, a, cv, bg, dw_w, dw_b, ln_g, ln_b, sc_w, w_out, post_g):
    b, s, _ = x.shape
    n_s = s // TS
    grid = (b, n_s)

    def main(width):
        return pl.BlockSpec((1, TS, width), lambda i, j: (i, j, 0))

    def prev(halo, width):
        per = TS // halo
        return pl.BlockSpec((1, halo, width), lambda i, j: (i, jnp.maximum(j * per - 1, 0), 0))

    def nxt(halo, width):
        per = TS // halo
        n_blocks = s // halo
        return pl.BlockSpec((1, halo, width),
                            lambda i, j: (i, jnp.minimum((j + 1) * per, n_blocks - 1), 0))

    return pl.pallas_call(
        _conv_mix_kernel,
        grid=grid,
        in_specs=[main(D_MODEL),
                  main(D_A), prev(HALO_A, D_A), nxt(HALO_A, D_A),
                  main(D_B), prev(HALO_B, D_B), nxt(HALO_B, D_B),
                  main(D_B),
                  _const_spec((CONV_A_WIDTH, SUBLANES, D_A)), _const_spec((1, D_A)),
                  _const_spec((1, D_A)), _const_spec((1, D_A)),
                  _const_spec((CONV_B_WIDTH, SUBLANES, D_B)),
                  _const_spec((D_A + D_B, D_MODEL)), _const_spec((1, D_MODEL))],
        out_specs=main(D_MODEL),
        out_shape=jax.ShapeDtypeStruct((b, s, D_MODEL), F32),
        scratch_shapes=[pltpu.VMEM((SUBLANES, TS + 2 * HALO_A, D_A), F32),
                        pltpu.VMEM((len(SHIFTS_B), TS + 2 * HALO_B, D_B), F32),
                        pltpu.VMEM((TS, D_A), F32),
                        pltpu.VMEM((TS, D_A + D_B), BF16)],
        compiler_params=_params(2),
        name="conv_mix",
    )(x, a, a, a, cv, cv, cv, bg,
      jnp.broadcast_to(dw_w[:, None, :], (CONV_A_WIDTH, SUBLANES, D_A)), dw_b, ln_g, ln_b,
      jnp.broadcast_to(sc_w[:, None, :], (CONV_B_WIDTH, SUBLANES, D_B)), w_out, post_g)


QD = GLA_HEADS * GLA_DK
VD = GLA_HEADS * GLA_DV


def _gla_in_kernel(x_ref, pre_ref, w_ref, wgate_ref, wa2_ref, ba2_ref,
                   q_ref, k_ref, v_ref, r_ref, la_ref):
    h = _rms(x_ref[...], pre_ref[...]).astype(BF16)
    q_ref[...] = _dot(h, w_ref[:, 0:QD]).astype(BF16)
    k_ref[...] = _dot(h, w_ref[:, QD:2 * QD]).astype(BF16)
    v_ref[...] = _dot(h, w_ref[:, 2 * QD:2 * QD + VD]).astype(BF16)
    r_ref[...] = _dot(h, w_ref[:, 2 * QD + VD:2 * QD + 2 * VD]).astype(BF16)
    gate = _dot(h, wgate_ref[...]).astype(BF16)
    z = _dot(gate, wa2_ref[...]) + ba2_ref[...]
    log_sig = jnp.minimum(z, 0.0) - jnp.log1p(jnp.exp(-jnp.abs(z)))
    la_ref[...] = log_sig / GATE_TAU


def _gla_in(x, pre_g, w_main, w_gate, wa2, ba2):
    t = x.shape[0]

    def rows(width):
        return pl.BlockSpec((TM, width), lambda i: (i, 0))

    def out(width, dtype):
        return jax.ShapeDtypeStruct((t, width), dtype)

    return pl.pallas_call(
        _gla_in_kernel,
        grid=(t // TM,),
        in_specs=[rows(D_MODEL), _const_spec((1, D_MODEL)),
                  _const_spec(w_main.shape), _const_spec((D_MODEL, GATE_PAD)),
                  _const_spec((GATE_PAD, 2 * QD)), _const_spec((1, 2 * QD))],
        out_specs=[rows(QD), rows(QD), rows(VD), rows(VD), rows(2 * QD)],
        out_shape=[out(QD, BF16), out(QD, BF16), out(VD, BF16), out(VD, BF16), out(2 * QD, F32)],
        compiler_params=_params(1),
        name="gla_in",
    )(x, pre_g, w_main, w_gate, wa2, ba2)


GLA_GROUP = 8


def _gla_kernel(q_ref, k_ref, v_ref, laf_ref, lab_ref, o_ref,
                qfb_ref, dec_ref, kv_ref, st_ref, cur_ref, oacc_ref):
    n_chunks = q_ref.shape[1] // CHUNK
    row = lax.broadcasted_iota(jnp.int32, (CHUNK, CHUNK), 0)
    col = lax.broadcasted_iota(jnp.int32, (CHUNK, CHUNK), 1)
    lower = row >= col
    upper = row <= col
    tri = lower.astype(BF16)
    tri3 = jnp.concatenate([tri, tri, tri], axis=1)
    scale = GLA_DK ** -0.5
    nt = (((1,), (1,)), ((), ()))
    tn = (((0,), (0,)), ((), ()))

    def chunk_rows(n):
        return pl.ds(pl.multiple_of(n * CHUNK, CHUNK), CHUNK)

    def group(i):
        return [i * GLA_GROUP + g for g in range(GLA_GROUP)]

    def front(i, carry):
        chunks = group(i)
        all_rows = [chunk_rows(n) for n in chunks]
        pres = []
        for rows in all_rows:
            la = jnp.concatenate([laf_ref[0, rows, :], lab_ref[0, rows, :]], axis=1)
            hi = la.astype(BF16)
            r1 = la - hi.astype(F32)
            mid = r1.astype(BF16)
            lo = (r1 - mid.astype(F32)).astype(BF16)
            pres.append(_dot(tri3, jnp.concatenate([hi, mid, lo], axis=0)))
        qf, qb, kf, kb, ke = [], [], [], [], []
        for n, rows, pre in zip(chunks, all_rows, pres):
            cum_f = pre[:, :GLA_DK]
            pre_b = pre[:, GLA_DK:]
            tot_f = cum_f[CHUNK - 1:CHUNK, :]
            tot_b = pre_b[CHUNK - 1:CHUNK, :]
            cum_b = tot_b - pre_b + lab_ref[0, rows, :]
            q = q_ref[0, rows, :].astype(F32) * scale
            k = k_ref[0, rows, :].astype(F32)
            kf32 = k * jnp.exp(-cum_f)
            kb32 = k * jnp.exp(-cum_b)
            dec = jnp.concatenate([jnp.exp(tot_f), jnp.exp(tot_b)], axis=1)
            qf.append((q * jnp.exp(cum_f)).astype(BF16))
            qb.append((q * jnp.exp(cum_b)).astype(BF16))
            kf.append(kf32.astype(BF16))
            kb.append(kb32.astype(BF16))
            ke.append(jnp.concatenate([kf32 * dec[:, :GLA_DK], kb32 * dec[:, GLA_DK:]],
                                      axis=1).astype(BF16))
            qfb_ref[rows, :GLA_DK] = qf[-1]
            qfb_ref[rows, GLA_DK:] = qb[-1]
            dec_ref[n] = jnp.broadcast_to(dec, dec_ref.shape[1:])
        s_f = [lax.dot_general(a, b, nt, preferred_element_type=F32) for a, b in zip(qf, kf)]
        s_b = [lax.dot_general(a, b, nt, preferred_element_type=F32) for a, b in zip(qb, kb)]
        for n, rows, k_end in zip(chunks, all_rows, ke):
            kv_ref[n] = lax.dot_general(v_ref[0, rows, :], k_end, tn,
                                        preferred_element_type=F32)
        for rows, sf, sb in zip(all_rows, s_f, s_b):
            scores = jnp.where(lower, sf, 0.0) + jnp.where(upper, sb, 0.0)
            oacc_ref[rows, :] = _dot(scores.astype(BF16), v_ref[0, rows, :])

    def scan(i, carry):
        n = i
        m = n_chunks - 1 - i
        st_ref[n, :, :GLA_DK] = cur_ref[:, :GLA_DK].astype(BF16)
        st_ref[m, :, GLA_DK:] = cur_ref[:, GLA_DK:].astype(BF16)
        cur_ref[:, :GLA_DK] = (cur_ref[:, :GLA_DK] * dec_ref[n][0:1, :GLA_DK]
                               + kv_ref[n][:, :GLA_DK])
        cur_ref[:, GLA_DK:] = (cur_ref[:, GLA_DK:] * dec_ref[m][0:1, GLA_DK:]
                               + kv_ref[m][:, GLA_DK:])

    def inter(i, carry):
        for n in group(i):
            rows = chunk_rows(n)
            o = oacc_ref[rows, :] + lax.dot_general(qfb_ref[rows, :], st_ref[n], nt,
                                                    preferred_element_type=F32)
            o_ref[0, rows, :] = o.astype(BF16)

    def run(body, trips):
        def step(i, carry):
            body(i, carry)
            return carry
        lax.fori_loop(0, trips, step, 0)

    n_groups = n_chunks // GLA_GROUP
    run(front, n_groups)
    cur_ref[...] = jnp.zeros_like(cur_ref)
    run(scan, n_chunks)
    run(inter, n_groups)


def _gla(q, k, v, la):
    b, s, _ = q.shape
    n_chunks = s // CHUNK

    def head(width, first_block=0):
        return pl.BlockSpec((1, s, width), lambda i, h: (i, 0, first_block + h))

    return pl.pallas_call(
        _gla_kernel,
        grid=(b, GLA_HEADS),
        in_specs=[head(GLA_DK), head(GLA_DK), head(GLA_DV), head(GLA_DK),
                  head(GLA_DK, GLA_HEADS)],
        out_specs=head(GLA_DV),
        out_shape=jax.ShapeDtypeStruct((b, s, VD), BF16),
        scratch_shapes=[pltpu.VMEM((s, 2 * GLA_DK), BF16),
                        pltpu.VMEM((n_chunks, 8, 2 * GLA_DK), F32),
                        pltpu.VMEM((n_chunks, GLA_DV, 2 * GLA_DK), F32),
                        pltpu.VMEM((n_chunks, GLA_DV, 2 * GLA_DK), BF16),
                        pltpu.VMEM((GLA_DV, 2 * GLA_DK), F32),
                        pltpu.VMEM((s, GLA_DV), F32)],
        compiler_params=_params(2),
        name="gla",
    )(q, k, v, la, la)


def _gla_out_kernel(x_ref, o_ref, r_ref, gn_ref, wo_ref, post_ref, y_ref, mbuf):
    for hd in range(GLA_HEADS):
        cols = slice(hd * GLA_DV, (hd + 1) * GLA_DV)
        o = o_ref[:, cols].astype(F32)
        o = o * lax.rsqrt(jnp.mean(o * o, axis=-1, keepdims=True) + EPS) * gn_ref[:, cols]
        r = r_ref[:, cols].astype(F32)
        mbuf[:, cols] = (o * (r * jax.nn.sigmoid(r))).astype(BF16)
    m = _dot(mbuf[...], wo_ref[...])
    y_ref[...] = x_ref[...] + _rms(m, post_ref[...])


def _gla_out(x, o, r, gn_g, w_out, post_g):
    t = x.shape[0]
    row = pl.BlockSpec((TM, D_MODEL), lambda i: (i, 0))
    return pl.pallas_call(
        _gla_out_kernel,
        grid=(t // TM,),
        in_specs=[row, row, row, _const_spec((1, VD)), _const_spec((VD, D_MODEL)),
                  _const_spec((1, D_MODEL))],
        out_specs=row,
        out_shape=jax.ShapeDtypeStruct((t, D_MODEL), F32),
        scratch_shapes=[pltpu.VMEM((TM, VD), BF16)],
        compiler_params=_params(1),
        name="gla_out",
    )(x, o, r, gn_g, w_out, post_g)


def kernel(x, mix_pre_g, mix_post_g, ffn_pre_g, ffn_post_g, cv_w_in, cv_dw_w, cv_dw_b, cv_ln_g,
           cv_ln_b, cv_sc_w, cv_w_out, gla_w_in, gla_wa2_f, gla_ba2_f, gla_wa2_b, gla_ba2_b,
           gla_gn_g, gla_w_out, ffn_w_gu, ffn_w_down):
    b, s, d = x.shape
    t = b * s

    def vec(g):
        return g.reshape(1, -1)

    def ffn(xt, layer):
        return _ffn(xt, vec(ffn_pre_g[layer]), ffn_w_gu[layer].astype(BF16),
                    ffn_w_down[layer].astype(BF16), vec(ffn_post_g[layer]))

    xt = x.reshape(t, d)

    a, cv, bg = _conv_in(xt, vec(mix_pre_g[0]), cv_w_in[0].astype(BF16))
    xs = _conv_mix(x, a.reshape(b, s, D_A), cv.reshape(b, s, D_B), bg.reshape(b, s, D_B),
                   cv_dw_w[0], vec(cv_dw_b[0]), vec(cv_ln_g[0]), vec(cv_ln_b[0]), cv_sc_w[0],
                   cv_w_out[0].astype(BF16), vec(mix_post_g[0]))
    xt = ffn(xs.reshape(t, d), 0)

    w_in = gla_w_in[0]
    n_main = 2 * QD + 2 * VD
    w_gate = jnp.pad(w_in[:, n_main:], ((0, 0), (0, GATE_PAD - 2 * GATE_RANK))).astype(BF16)
    wa2 = jnp.zeros((GATE_PAD, 2 * QD), F32)
    wa2 = wa2.at[:GATE_RANK, :QD].set(gla_wa2_f[0])
    wa2 = wa2.at[GATE_RANK:2 * GATE_RANK, QD:].set(gla_wa2_b[0])
    ba2 = jnp.concatenate([gla_ba2_f[0], gla_ba2_b[0]]).reshape(1, 2 * QD)
    q, k, v, r, la = _gla_in(xt, vec(mix_pre_g[1]), w_in.astype(BF16), w_gate,
                             wa2.astype(BF16), ba2)
    o = _gla(q.reshape(b, s, QD), k.reshape(b, s, QD), v.reshape(b, s, VD),
             la.reshape(b, s, 2 * QD))
    xt = _gla_out(xt, o.reshape(t, VD), r, vec(gla_gn_g[0]), gla_w_out[0].astype(BF16),
                  vec(mix_post_g[1]))
    xt = ffn(xt, 1)
    return xt.reshape(b, s, d)
```

```python
import jax
import jax.numpy as jnp
from jax import lax
from jax.experimental import pallas as pl
from jax.experimental.pallas import tpu as pltpu

D_MODEL = 1024
D_A = 512
D_B = 512
CONV_A_WIDTH = 31
CONV_B_WIDTH = 3
GLA_HEADS = 4
GLA_DK = 128
GLA_DV = 256
GATE_RANK = 16
GATE_TAU = 16.0
CHUNK = 64
D_FF = 2816
EPS = 1e-6

HALO_A = 16
HALO_B = 8
GATE_PAD = 128

TM = 512
TS = 512
VMEM_LIMIT = 56 * 1024 * 1024

F32 = jnp.float32
BF16 = jnp.bfloat16


def _rms(x, g):
    return x * lax.rsqrt(jnp.mean(x * x, axis=-1, keepdims=True) + EPS) * g


def _dot(a, b):
    return jnp.dot(a, b, preferred_element_type=F32)


def _const_spec(shape):
    zeros = (0,) * len(shape)
    return pl.BlockSpec(shape, lambda *_: zeros, pipeline_mode=pl.Buffered(1))


def _params(n_grid):
    return pltpu.CompilerParams(dimension_semantics=("arbitrary",) * n_grid,
                                vmem_limit_bytes=VMEM_LIMIT)


FF_CHUNK = 256
FF_TM = 1024
FF_SUB = 512


def _ffn_kernel(x_ref, m_ref, wo_ref, mpost_ref, pre_ref, wgu_ref, wd_ref, post_ref, o_ref,
                act_ref):
    subs = [slice(i * FF_SUB, (i + 1) * FF_SUB) for i in range(FF_TM // FF_SUB)]
    mixed = [_dot(m_ref[rows, :], wo_ref[...]) for rows in subs]
    hidden = []
    for rows, mo in zip(subs, mixed):
        x = x_ref[rows, :] + _rms(mo, mpost_ref[...])
        o_ref[rows, :] = x
        hidden.append(_rms(x, pre_ref[...]).astype(BF16))
    for rows, h in zip(subs, hidden):
        for c in range(D_FF // FF_CHUNK):
            lo = c * FF_CHUNK
            g = _dot(h, wgu_ref[:, lo:lo + FF_CHUNK])
            u = _dot(h, wgu_ref[:, D_FF + lo:D_FF + lo + FF_CHUNK])
            act_ref[rows, lo:lo + FF_CHUNK] = (g * jax.nn.sigmoid(g) * u).astype(BF16)
    down = [_dot(act_ref[rows, :], wd_ref[...]) for rows in subs]
    for rows, f in zip(subs, down):
        o_ref[rows, :] = o_ref[rows, :] + _rms(f, post_ref[...])


def _ffn(x, m, w_out, mix_post_g, pre_g, w_gu, w_d, post_g):
    t = x.shape[0]
    row = pl.BlockSpec((FF_TM, D_MODEL), lambda i: (i, 0))
    return pl.pallas_call(
        _ffn_kernel,
        grid=(t // FF_TM,),
        in_specs=[row, row, _const_spec((D_MODEL, D_MODEL)), _const_spec((1, D_MODEL)),
                  _const_spec((1, D_MODEL)), _const_spec((D_MODEL, 2 * D_FF)),
                  _const_spec((D_FF, D_MODEL)), _const_spec((1, D_MODEL))],
        out_specs=row,
        out_shape=jax.ShapeDtypeStruct((t, D_MODEL), F32),
        scratch_shapes=[pltpu.VMEM((FF_TM, D_FF), BF16)],
        compiler_params=_params(1),
        name="ffn",
    )(x, m, w_out, mix_post_g, pre_g, w_gu, w_d, post_g)


def _conv_in_kernel(x_ref, pre_ref, w_ref, a_ref, cv_ref, bg_ref):
    h = _rms(x_ref[...], pre_ref[...]).astype(BF16)

    def proj(j):
        return _dot(h, w_ref[:, j * D_A:(j + 1) * D_A])

    a_ref[...] = proj(0) * jax.nn.sigmoid(proj(1))
    bg_ref[...] = proj(2)
    cv_ref[...] = proj(3) * proj(4)


def _conv_in(x, pre_g, w_in):
    t = x.shape[0]
    row = pl.BlockSpec((TM, D_MODEL), lambda i: (i, 0))
    half = pl.BlockSpec((TM, D_A), lambda i: (i, 0))
    out = jax.ShapeDtypeStruct((t, D_A), F32)
    return pl.pallas_call(
        _conv_in_kernel,
        grid=(t // TM,),
        in_specs=[row, _const_spec((1, D_MODEL)), _const_spec((D_MODEL, 5 * D_A))],
        out_specs=[half, half, half],
        out_shape=[out, out, out],
        compiler_params=_params(1),
        name="conv_in",
    )(x, pre_g, w_in)


CONV_ROWS = 32
CONV_LANES = 256
NORM_ROWS = 32
NORM_UNROLL = 8
SUBLANES = 8
PAD_A = (CONV_A_WIDTH - 1) // 2
PAD_B = (CONV_B_WIDTH - 1) // 2
SHIFTS_B = sorted({(HALO_B + k - PAD_B) % SUBLANES for k in range(CONV_B_WIDTH)})


def _conv_mix_kernel(a_ref, ap_ref, an_ref, cv_ref, cp_ref, cn_ref, bg_ref,
                     dww_ref, dwb_ref, lng_ref, lnb_ref, scw_ref, m_ref, ash, csh, conv_ref):
    j = pl.program_id(1)
    first = j == 0
    last = j == pl.num_programs(1) - 1
    ash[0, 0:HALO_A, :] = jnp.where(first, 0.0, ap_ref[0])
    ash[0, HALO_A:HALO_A + TS, :] = a_ref[0]
    ash[0, HALO_A + TS:, :] = jnp.where(last, 0.0, an_ref[0])
    csh[0, 0:HALO_B, :] = jnp.where(first, 0.0, cp_ref[0])
    csh[0, HALO_B:HALO_B + TS, :] = cv_ref[0]
    csh[0, HALO_B + TS:, :] = jnp.where(last, 0.0, cn_ref[0])
    rows_a = TS + 2 * HALO_A - SUBLANES
    for r in range(1, SUBLANES):
        ash[r, 0:rows_a, :] = ash[0, r:r + rows_a, :]
    rows_b = TS + 2 * HALO_B - SUBLANES
    for i, r in enumerate(SHIFTS_B):
        if r:
            csh[i, 0:rows_b, :] = csh[0, r:r + rows_b, :]

    groups = CONV_ROWS // SUBLANES

    for cb in range(D_A // CONV_LANES):
        lanes = slice(cb * CONV_LANES, (cb + 1) * CONV_LANES)

        def conv_block(i, carry, lanes=lanes):
            r0 = pl.multiple_of(i * CONV_ROWS, CONV_ROWS)
            acc = None
            for r in range(SUBLANES):
                taps = [k for k in range(CONV_A_WIDTH) if (HALO_A + k - PAD_A) % SUBLANES == r]
                offs = [HALO_A + k - PAD_A - r for k in taps]
                lo = min(offs)
                window = ash[r, pl.ds(r0 + lo, max(offs) - lo + CONV_ROWS), lanes]
                for k, off in zip(taps, offs):
                    slab = window[off - lo:off - lo + CONV_ROWS]
                    term = slab.reshape(groups, SUBLANES, CONV_LANES) * dww_ref[k, :, lanes][None]
                    acc = term if acc is None else acc + term
            conv_ref[pl.ds(r0, CONV_ROWS), lanes] = acc.reshape(CONV_ROWS, CONV_LANES)
            return carry

        lax.fori_loop(0, TS // CONV_ROWS, conv_block, 0)

    def norm_block(r0):
        rows = pl.ds(r0, NORM_ROWS)
        acc = conv_ref[rows, :] + dwb_ref[...]
        mu = jnp.mean(acc, axis=-1, keepdims=True)
        xc = acc - mu
        y = xc * lax.rsqrt(jnp.mean(xc * xc, axis=-1, keepdims=True) + EPS)
        y = y * lng_ref[...] + lnb_ref[...]
        m_ref[0, rows, 0:D_A] = (y * jax.nn.sigmoid(y)).astype(BF16)

        accb = None
        for k in range(CONV_B_WIDTH):
            off = HALO_B + k - PAD_B
            slab = csh[SHIFTS_B.index(off % SUBLANES), pl.ds(r0 + off - off % SUBLANES, NORM_ROWS), :]
            term = slab.reshape(NORM_ROWS // SUBLANES, SUBLANES, D_B) * scw_ref[k][None]
            accb = term if accb is None else accb + term
        m_ref[0, rows, D_A:] = (bg_ref[0, rows, :] * accb.reshape(NORM_ROWS, D_B)).astype(BF16)

    def norm_step(i, carry):
        for u in range(NORM_UNROLL):
            norm_block(pl.multiple_of((i * NORM_UNROLL + u) * NORM_ROWS, NORM_ROWS))
        return carry

    lax.fori_loop(0, TS // (NORM_ROWS * NORM_UNROLL), norm_step, 0)


def _conv_mix(a, cv, bg, dw_w, dw_b, ln_g, ln_b, sc_w):
    b, s, _ = a.shape
    n_s = s // TS
    grid = (b, n_s)

    def main(width):
        return pl.BlockSpec((1, TS, width), lambda i, j: (i, j, 0))

    def prev(halo, width):
        per = TS // halo
        return pl.BlockSpec((1, halo, width), lambda i, j: (i, jnp.maximum(j * per - 1, 0), 0))

    def nxt(halo, width):
        per = TS // halo
        n_blocks = s // halo
        return pl.BlockSpec((1, halo, width),
                            lambda i, j: (i, jnp.minimum((j + 1) * per, n_blocks - 1), 0))

    return pl.pallas_call(
        _conv_mix_kernel,
        grid=grid,
        in_specs=[main(D_A), prev(HALO_A, D_A), nxt(HALO_A, D_A),
                  main(D_B), prev(HALO_B, D_B), nxt(HALO_B, D_B),
                  main(D_B),
                  _const_spec((CONV_A_WIDTH, SUBLANES, D_A)), _const_spec((1, D_A)),
                  _const_spec((1, D_A)), _const_spec((1, D_A)),
                  _const_spec((CONV_B_WIDTH, SUBLANES, D_B))],
        out_specs=main(D_A + D_B),
        out_shape=jax.ShapeDtypeStruct((b, s, D_A + D_B), BF16),
        scratch_shapes=[pltpu.VMEM((SUBLANES, TS + 2 * HALO_A, D_A), F32),
                        pltpu.VMEM((len(SHIFTS_B), TS + 2 * HALO_B, D_B), F32),
                        pltpu.VMEM((TS, D_A), F32)],
        compiler_params=_params(2),
        name="conv_mix",
    )(a, a, a, cv, cv, cv, bg,
      jnp.broadcast_to(dw_w[:, None, :], (CONV_A_WIDTH, SUBLANES, D_A)), dw_b, ln_g, ln_b,
      jnp.broadcast_to(sc_w[:, None, :], (CONV_B_WIDTH, SUBLANES, D_B)))


QD = GLA_HEADS * GLA_DK
VD = GLA_HEADS * GLA_DV


def _gla_in_kernel(x_ref, pre_ref, w_ref, wgate_ref, wa2_ref, ba2_ref,
                   q_ref, k_ref, v_ref, r_ref, la_ref):
    h = _rms(x_ref[...], pre_ref[...]).astype(BF16)
    gate = _dot(h, wgate_ref[...]).astype(BF16)
    z = _dot(gate, wa2_ref[...]) + ba2_ref[...]
    log_sig = jnp.minimum(z, 0.0) - jnp.log(1.0 + jnp.exp(-jnp.abs(z)))
    la_ref[...] = log_sig * (1.0 / GATE_TAU)
    q_ref[...] = _dot(h, w_ref[:, 0:QD]).astype(BF16)
    k_ref[...] = _dot(h, w_ref[:, QD:2 * QD]).astype(BF16)
    v_ref[...] = _dot(h, w_ref[:, 2 * QD:2 * QD + VD]).astype(BF16)
    r_ref[...] = _dot(h, w_ref[:, 2 * QD + VD:2 * QD + 2 * VD]).astype(BF16)


def _gla_in(x, pre_g, w_main, w_gate, wa2, ba2):
    t = x.shape[0]

    def rows(width):
        return pl.BlockSpec((TM, width), lambda i: (i, 0))

    def out(width, dtype):
        return jax.ShapeDtypeStruct((t, width), dtype)

    return pl.pallas_call(
        _gla_in_kernel,
        grid=(t // TM,),
        in_specs=[rows(D_MODEL), _const_spec((1, D_MODEL)),
                  _const_spec(w_main.shape), _const_spec((D_MODEL, GATE_PAD)),
                  _const_spec((GATE_PAD, 2 * QD)), _const_spec((1, 2 * QD))],
        out_specs=[rows(QD), rows(QD), rows(VD), rows(VD), rows(2 * QD)],
        out_shape=[out(QD, BF16), out(QD, BF16), out(VD, BF16), out(VD, BF16), out(2 * QD, F32)],
        compiler_params=_params(1),
        name="gla_in",
    )(x, pre_g, w_main, w_gate, wa2, ba2)


GLA_GROUP = 8


def _gla_kernel(q_ref, k_ref, v_ref, laf_ref, lab_ref, r_ref, gn_ref, m_ref,
                qfb_ref, dec_ref, kv_ref, st_ref, cur_ref, oacc_ref):
    n_chunks = q_ref.shape[1] // CHUNK
    row = lax.broadcasted_iota(jnp.int32, (CHUNK, CHUNK), 0)
    col = lax.broadcasted_iota(jnp.int32, (CHUNK, CHUNK), 1)
    lower = row >= col
    upper = row <= col
    tri = lower.astype(BF16)
    tri3 = jnp.concatenate([tri, tri, tri], axis=1)
    scale = GLA_DK ** -0.5
    nt = (((1,), (1,)), ((), ()))
    tn = (((0,), (0,)), ((), ()))

    def chunk_rows(n):
        return pl.ds(pl.multiple_of(n * CHUNK, CHUNK), CHUNK)

    def group(i):
        return [i * GLA_GROUP + g for g in range(GLA_GROUP)]

    def front(i, carry):
        chunks = group(i)
        all_rows = [chunk_rows(n) for n in chunks]
        pres = []
        for rows in all_rows:
            la = jnp.concatenate([laf_ref[0, rows, :], lab_ref[0, rows, :]], axis=1)
            hi = la.astype(BF16)
            r1 = la - hi.astype(F32)
            mid = r1.astype(BF16)
            lo = (r1 - mid.astype(F32)).astype(BF16)
            pres.append(_dot(tri3, jnp.concatenate([hi, mid, lo], axis=0)))
        qf, qb, kf, kb, ke = [], [], [], [], []
        for n, rows, pre in zip(chunks, all_rows, pres):
            cum_f = pre[:, :GLA_DK]
            pre_b = pre[:, GLA_DK:]
            tot_f = cum_f[CHUNK - 1:CHUNK, :]
            tot_b = pre_b[CHUNK - 1:CHUNK, :]
            cum_b = tot_b - pre_b + lab_ref[0, rows, :]
            q = q_ref[0, rows, :].astype(F32) * scale
            k = k_ref[0, rows, :].astype(F32)
            kf32 = k * jnp.exp(-cum_f)
            kb32 = k * jnp.exp(-cum_b)
            dec = jnp.concatenate([jnp.exp(tot_f), jnp.exp(tot_b)], axis=1)
            qf.append((q * jnp.exp(cum_f)).astype(BF16))
            qb.append((q * jnp.exp(cum_b)).astype(BF16))
            kf.append(kf32.astype(BF16))
            kb.append(kb32.astype(BF16))
            ke.append(jnp.concatenate([kf32 * dec[:, :GLA_DK], kb32 * dec[:, GLA_DK:]],
                                      axis=1).astype(BF16))
            qfb_ref[rows, :GLA_DK] = qf[-1]
            qfb_ref[rows, GLA_DK:] = qb[-1]
            dec_ref[n] = jnp.broadcast_to(dec, dec_ref.shape[1:])
        s_f = [lax.dot_general(a, b, nt, preferred_element_type=F32) for a, b in zip(qf, kf)]
        s_b = [lax.dot_general(a, b, nt, preferred_element_type=F32) for a, b in zip(qb, kb)]
        for n, rows, k_end in zip(chunks, all_rows, ke):
            kv_ref[n] = lax.dot_general(v_ref[0, rows, :], k_end, tn,
                                        preferred_element_type=F32)
        for rows, sf, sb in zip(all_rows, s_f, s_b):
            scores = jnp.where(lower, sf, 0.0) + jnp.where(upper, sb, 0.0)
            oacc_ref[rows, :] = _dot(scores.astype(BF16), v_ref[0, rows, :])

    def scan(i, carry):
        n = i
        m = n_chunks - 1 - i
        st_ref[n, :, :GLA_DK] = cur_ref[:, :GLA_DK].astype(BF16)
        st_ref[m, :, GLA_DK:] = cur_ref[:, GLA_DK:].astype(BF16)
        cur_ref[:, :GLA_DK] = (cur_ref[:, :GLA_DK] * dec_ref[n][0:1, :GLA_DK]
                               + kv_ref[n][:, :GLA_DK])
        cur_ref[:, GLA_DK:] = (cur_ref[:, GLA_DK:] * dec_ref[m][0:1, GLA_DK:]
                               + kv_ref[m][:, GLA_DK:])

    def inter(i, carry):
        for n in group(i):
            rows = chunk_rows(n)
            o = oacc_ref[rows, :] + lax.dot_general(qfb_ref[rows, :], st_ref[n], nt,
                                                    preferred_element_type=F32)
            o = o * lax.rsqrt(jnp.mean(o * o, axis=-1, keepdims=True) + EPS) * gn_ref[...]
            r = r_ref[0, rows, :].astype(F32)
            m_ref[0, rows, :] = (o * (r * jax.nn.sigmoid(r))).astype(BF16)

    def run(body, trips):
        def step(i, carry):
            body(i, carry)
            return carry
        lax.fori_loop(0, trips, step, 0)

    n_groups = n_chunks // GLA_GROUP
    run(front, n_groups)
    cur_ref[...] = jnp.zeros_like(cur_ref)
    run(scan, n_chunks)
    run(inter, n_groups)


def _gla(q, k, v, la, r, gn_g):
    b, s, _ = q.shape
    n_chunks = s // CHUNK

    def head(width, first_block=0):
        return pl.BlockSpec((1, s, width), lambda i, h: (i, 0, first_block + h))

    return pl.pallas_call(
        _gla_kernel,
        grid=(b, GLA_HEADS),
        in_specs=[head(GLA_DK), head(GLA_DK), head(GLA_DV), head(GLA_DK),
                  head(GLA_DK, GLA_HEADS), head(GLA_DV),
                  pl.BlockSpec((1, GLA_DV), lambda i, h: (0, h))],
        out_specs=head(GLA_DV),
        out_shape=jax.ShapeDtypeStruct((b, s, VD), BF16),
        scratch_shapes=[pltpu.VMEM((s, 2 * GLA_DK), BF16),
                        pltpu.VMEM((n_chunks, 8, 2 * GLA_DK), F32),
                        pltpu.VMEM((n_chunks, GLA_DV, 2 * GLA_DK), F32),
                        pltpu.VMEM((n_chunks, GLA_DV, 2 * GLA_DK), BF16),
                        pltpu.VMEM((GLA_DV, 2 * GLA_DK), F32),
                        pltpu.VMEM((s, GLA_DV), F32)],
        compiler_params=_params(2),
        name="gla",
    )(q, k, v, la, la, r, gn_g)


def kernel(x, mix_pre_g, mix_post_g, ffn_pre_g, ffn_post_g, cv_w_in, cv_dw_w, cv_dw_b, cv_ln_g,
           cv_ln_b, cv_sc_w, cv_w_out, gla_w_in, gla_wa2_f, gla_ba2_f, gla_wa2_b, gla_ba2_b,
           gla_gn_g, gla_w_out, ffn_w_gu, ffn_w_down):
    b, s, d = x.shape
    t = b * s

    def vec(g):
        return g.reshape(1, -1)

    def mix_out_ffn(xt, m, w_out, layer):
        return _ffn(xt, m, w_out.astype(BF16), vec(mix_post_g[layer]), vec(ffn_pre_g[layer]),
                    ffn_w_gu[layer].astype(BF16), ffn_w_down[layer].astype(BF16),
                    vec(ffn_post_g[layer]))

    xt = x.reshape(t, d)

    a, cv, bg = _conv_in(xt, vec(mix_pre_g[0]), cv_w_in[0].astype(BF16))
    m = _conv_mix(a.reshape(b, s, D_A), cv.reshape(b, s, D_B), bg.reshape(b, s, D_B),
                  cv_dw_w[0], vec(cv_dw_b[0]), vec(cv_ln_g[0]), vec(cv_ln_b[0]), cv_sc_w[0])
    xt = mix_out_ffn(xt, m.reshape(t, d), cv_w_out[0], 0)

    w_in = gla_w_in[0]
    n_main = 2 * QD + 2 * VD
    w_gate = jnp.pad(w_in[:, n_main:], ((0, 0), (0, GATE_PAD - 2 * GATE_RANK))).astype(BF16)
    wa2 = jnp.zeros((GATE_PAD, 2 * QD), F32)
    wa2 = wa2.at[:GATE_RANK, :QD].set(gla_wa2_f[0])
    wa2 = wa2.at[GATE_RANK:2 * GATE_RANK, QD:].set(gla_wa2_b[0])
    ba2 = jnp.concatenate([gla_ba2_f[0], gla_ba2_b[0]]).reshape(1, 2 * QD)
    q, k, v, r, la = _gla_in(xt, vec(mix_pre_g[1]), w_in.astype(BF16), w_gate,
                             wa2.astype(BF16), ba2)
    m = _gla(q.reshape(b, s, QD), k.reshape(b, s, QD), v.reshape(b, s, VD),
             la.reshape(b, s, 2 * QD), r.reshape(b, s, VD), vec(gla_gn_g[0]))
    xt = mix_out_ffn(xt, m.reshape(t, d), gla_w_out[0], 1)
    return xt.reshape(b, s, d)
```

```python
import jax
import jax.numpy as jnp
from jax import lax
from jax.experimental import pallas as pl
from jax.experimental.pallas import tpu as pltpu

D_MODEL = 1024
D_A = 512
D_B = 512
CONV_A_WIDTH = 31
CONV_B_WIDTH = 3
GLA_HEADS = 4
GLA_DK = 128
GLA_DV = 256
GATE_RANK = 16
GATE_TAU = 16.0
CHUNK = 64
D_FF = 2816
EPS = 1e-6

HALO_A = 16
HALO_B = 8
GATE_PAD = 128

PROJ_TM = 1024
PROJ_SUB = 512
TS = 512
VMEM_LIMIT = 56 * 1024 * 1024

F32 = jnp.float32
BF16 = jnp.bfloat16


def _rms(x, g):
    return x * lax.rsqrt(jnp.mean(x * x, axis=-1, keepdims=True) + EPS) * g


def _dot(a, b):
    return jnp.dot(a, b, preferred_element_type=F32)


def _const_spec(shape):
    zeros = (0,) * len(shape)
    return pl.BlockSpec(shape, lambda *_: zeros, pipeline_mode=pl.Buffered(1))


def _params(n_grid):
    return pltpu.CompilerParams(dimension_semantics=("arbitrary",) * n_grid,
                                vmem_limit_bytes=VMEM_LIMIT)


def _sub_tiles(tile, sub):
    return [slice(i * sub, (i + 1) * sub) for i in range(tile // sub)]


FF_CHUNK = 256
FF_TM = 1024
FF_SUB = 512


def _ffn_kernel(x_ref, m_ref, wo_ref, mpost_ref, pre_ref, wgu_ref, wd_ref, post_ref, o_ref,
                act_ref):
    subs = _sub_tiles(FF_TM, FF_SUB)
    mixed = [_dot(m_ref[rows, :], wo_ref[...]) for rows in subs]
    hidden = []
    for rows, mo in zip(subs, mixed):
        x = x_ref[rows, :] + _rms(mo, mpost_ref[...])
        o_ref[rows, :] = x
        hidden.append(_rms(x, pre_ref[...]).astype(BF16))
    for rows, h in zip(subs, hidden):
        for c in range(D_FF // FF_CHUNK):
            lo = c * FF_CHUNK
            g = _dot(h, wgu_ref[:, lo:lo + FF_CHUNK])
            u = _dot(h, wgu_ref[:, D_FF + lo:D_FF + lo + FF_CHUNK])
            act_ref[rows, lo:lo + FF_CHUNK] = (g * jax.nn.sigmoid(g) * u).astype(BF16)
    down = [_dot(act_ref[rows, :], wd_ref[...]) for rows in subs]
    for rows, f in zip(subs, down):
        o_ref[rows, :] = o_ref[rows, :] + _rms(f, post_ref[...])


def _ffn(x, m, w_out, mix_post_g, pre_g, w_gu, w_d, post_g):
    t = x.shape[0]
    row = pl.BlockSpec((FF_TM, D_MODEL), lambda i: (i, 0))
    return pl.pallas_call(
        _ffn_kernel,
        grid=(t // FF_TM,),
        in_specs=[row, row, _const_spec((D_MODEL, D_MODEL)), _const_spec((1, D_MODEL)),
                  _const_spec((1, D_MODEL)), _const_spec((D_MODEL, 2 * D_FF)),
                  _const_spec((D_FF, D_MODEL)), _const_spec((1, D_MODEL))],
        out_specs=row,
        out_shape=jax.ShapeDtypeStruct((t, D_MODEL), F32),
        scratch_shapes=[pltpu.VMEM((FF_TM, D_FF), BF16)],
        compiler_params=_params(1),
        name="ffn",
    )(x, m, w_out, mix_post_g, pre_g, w_gu, w_d, post_g)


def _conv_in_kernel(x_ref, pre_ref, w_ref, a_ref, cv_ref, bg_ref):
    subs = _sub_tiles(PROJ_TM, PROJ_SUB)
    hidden = [_rms(x_ref[rows, :], pre_ref[...]).astype(BF16) for rows in subs]
    for rows, h in zip(subs, hidden):
        def proj(j, h=h):
            return _dot(h, w_ref[:, j * D_A:(j + 1) * D_A])

        a_ref[rows, :] = proj(0) * jax.nn.sigmoid(proj(1))
        bg_ref[rows, :] = proj(2)
        cv_ref[rows, :] = proj(3) * proj(4)


def _conv_in(x, pre_g, w_in):
    t = x.shape[0]
    row = pl.BlockSpec((PROJ_TM, D_MODEL), lambda i: (i, 0))
    half = pl.BlockSpec((PROJ_TM, D_A), lambda i: (i, 0))
    out = jax.ShapeDtypeStruct((t, D_A), F32)
    return pl.pallas_call(
        _conv_in_kernel,
        grid=(t // PROJ_TM,),
        in_specs=[row, _const_spec((1, D_MODEL)), _const_spec((D_MODEL, 5 * D_A))],
        out_specs=[half, half, half],
        out_shape=[out, out, out],
        compiler_params=_params(1),
        name="conv_in",
    )(x, pre_g, w_in)


CONV_ROWS = 256
CONV_LANES = 128
NORM_ROWS = 32
NORM_UNROLL = 8
SUBLANES = 8
PAD_A = (CONV_A_WIDTH - 1) // 2
PAD_B = (CONV_B_WIDTH - 1) // 2
SHIFTS_B = sorted({(HALO_B + k - PAD_B) % SUBLANES for k in range(CONV_B_WIDTH)})


def _conv_mix_kernel(a_ref, ap_ref, an_ref, cv_ref, cp_ref, cn_ref, bg_ref,
                     dww_ref, dwb_ref, lng_ref, lnb_ref, scw_ref, m_ref, ash, csh, conv_ref):
    j = pl.program_id(1)
    first = j == 0
    last = j == pl.num_programs(1) - 1
    ash[0, 0:HALO_A, :] = jnp.where(first, 0.0, ap_ref[0])
    ash[0, HALO_A:HALO_A + TS, :] = a_ref[0]
    ash[0, HALO_A + TS:, :] = jnp.where(last, 0.0, an_ref[0])
    csh[0, 0:HALO_B, :] = jnp.where(first, 0.0, cp_ref[0])
    csh[0, HALO_B:HALO_B + TS, :] = cv_ref[0]
    csh[0, HALO_B + TS:, :] = jnp.where(last, 0.0, cn_ref[0])
    rows_a = TS + 2 * HALO_A - SUBLANES
    for r in range(1, SUBLANES):
        ash[r, 0:rows_a, :] = ash[0, r:r + rows_a, :]
    rows_b = TS + 2 * HALO_B - SUBLANES
    for i, r in enumerate(SHIFTS_B):
        if r:
            csh[i, 0:rows_b, :] = csh[0, r:r + rows_b, :]

    groups = CONV_ROWS // SUBLANES

    for cb in range(D_A // CONV_LANES):
        lanes = slice(cb * CONV_LANES, (cb + 1) * CONV_LANES)

        def conv_block(i, carry, lanes=lanes):
            r0 = pl.multiple_of(i * CONV_ROWS, CONV_ROWS)
            acc = None
            for r in range(SUBLANES):
                taps = [k for k in range(CONV_A_WIDTH) if (HALO_A + k - PAD_A) % SUBLANES == r]
                offs = [HALO_A + k - PAD_A - r for k in taps]
                lo = min(offs)
                window = ash[r, pl.ds(r0 + lo, max(offs) - lo + CONV_ROWS), lanes]
                for k, off in zip(taps, offs):
                    slab = window[off - lo:off - lo + CONV_ROWS]
                    term = slab.reshape(groups, SUBLANES, CONV_LANES) * dww_ref[k, :, lanes][None]
                    acc = term if acc is None else acc + term
            conv_ref[pl.ds(r0, CONV_ROWS), lanes] = acc.reshape(CONV_ROWS, CONV_LANES)
            return carry

        lax.fori_loop(0, TS // CONV_ROWS, conv_block, 0)

    def norm_block(r0):
        rows = pl.ds(r0, NORM_ROWS)
        acc = conv_ref[rows, :] + dwb_ref[...]
        mu = jnp.mean(acc, axis=-1, keepdims=True)
        xc = acc - mu
        y = xc * lax.rsqrt(jnp.mean(xc * xc, axis=-1, keepdims=True) + EPS)
        y = y * lng_ref[...] + lnb_ref[...]
        m_ref[0, rows, 0:D_A] = (y * jax.nn.sigmoid(y)).astype(BF16)

        accb = None
        for k in range(CONV_B_WIDTH):
            off = HALO_B + k - PAD_B
            slab = csh[SHIFTS_B.index(off % SUBLANES), pl.ds(r0 + off - off % SUBLANES, NORM_ROWS), :]
            term = slab.reshape(NORM_ROWS // SUBLANES, SUBLANES, D_B) * scw_ref[k][None]
            accb = term if accb is None else accb + term
        m_ref[0, rows, D_A:] = (bg_ref[0, rows, :] * accb.reshape(NORM_ROWS, D_B)).astype(BF16)

    def norm_step(i, carry):
        for u in range(NORM_UNROLL):
            norm_block(pl.multiple_of((i * NORM_UNROLL + u) * NORM_ROWS, NORM_ROWS))
        return carry

    lax.fori_loop(0, TS // (NORM_ROWS * NORM_UNROLL), norm_step, 0)


def _conv_mix(a, cv, bg, dw_w, dw_b, ln_g, ln_b, sc_w):
    b, s, _ = a.shape
    n_s = s // TS
    grid = (b, n_s)

    def main(width):
        return pl.BlockSpec((1, TS, width), lambda i, j: (i, j, 0))

    def prev(halo, width):
        per = TS // halo
        return pl.BlockSpec((1, halo, width), lambda i, j: (i, jnp.maximum(j * per - 1, 0), 0))

    def nxt(halo, width):
        per = TS // halo
        n_blocks = s // halo
        return pl.BlockSpec((1, halo, width),
                            lambda i, j: (i, jnp.minimum((j + 1) * per, n_blocks - 1), 0))

    return pl.pallas_call(
        _conv_mix_kernel,
        grid=grid,
        in_specs=[main(D_A), prev(HALO_A, D_A), nxt(HALO_A, D_A),
                  main(D_B), prev(HALO_B, D_B), nxt(HALO_B, D_B),
                  main(D_B),
                  _const_spec((CONV_A_WIDTH, SUBLANES, D_A)), _const_spec((1, D_A)),
                  _const_spec((1, D_A)), _const_spec((1, D_A)),
                  _const_spec((CONV_B_WIDTH, SUBLANES, D_B))],
        out_specs=main(D_A + D_B),
        out_shape=jax.ShapeDtypeStruct((b, s, D_A + D_B), BF16),
        scratch_shapes=[pltpu.VMEM((SUBLANES, TS + 2 * HALO_A, D_A), F32),
                        pltpu.VMEM((len(SHIFTS_B), TS + 2 * HALO_B, D_B), F32),
                        pltpu.VMEM((TS, D_A), F32)],
        compiler_params=_params(2),
        name="conv_mix",
    )(a, a, a, cv, cv, cv, bg,
      jnp.broadcast_to(dw_w[:, None, :], (CONV_A_WIDTH, SUBLANES, D_A)), dw_b, ln_g, ln_b,
      jnp.broadcast_to(sc_w[:, None, :], (CONV_B_WIDTH, SUBLANES, D_B)))


QD = GLA_HEADS * GLA_DK
VD = GLA_HEADS * GLA_DV


def _gla_in_kernel(x_ref, pre_ref, w_ref, wgate_ref, wa2_ref, ba2_ref,
                   q_ref, k_ref, v_ref, r_ref, la_ref):
    subs = _sub_tiles(PROJ_TM, PROJ_SUB)
    hidden = [_rms(x_ref[rows, :], pre_ref[...]).astype(BF16) for rows in subs]
    for rows, h in zip(subs, hidden):
        gate = _dot(h, wgate_ref[...]).astype(BF16)
        z = _dot(gate, wa2_ref[...]) + ba2_ref[...]
        log_sig = jnp.minimum(z, 0.0) - jnp.log(1.0 + jnp.exp(-jnp.abs(z)))
        la_ref[rows, :] = log_sig * (1.0 / GATE_TAU)
        q_ref[rows, :] = _dot(h, w_ref[:, 0:QD]).astype(BF16)
        k_ref[rows, :] = _dot(h, w_ref[:, QD:2 * QD]).astype(BF16)
        v_ref[rows, :] = _dot(h, w_ref[:, 2 * QD:2 * QD + VD]).astype(BF16)
        r_ref[rows, :] = _dot(h, w_ref[:, 2 * QD + VD:2 * QD + 2 * VD]).astype(BF16)


def _gla_in(x, pre_g, w_main, w_gate, wa2, ba2):
    t = x.shape[0]

    def rows(width):
        return pl.BlockSpec((PROJ_TM, width), lambda i: (i, 0))

    def out(width, dtype):
        return jax.ShapeDtypeStruct((t, width), dtype)

    return pl.pallas_call(
        _gla_in_kernel,
        grid=(t // PROJ_TM,),
        in_specs=[rows(D_MODEL), _const_spec((1, D_MODEL)),
                  _const_spec(w_main.shape), _const_spec((D_MODEL, GATE_PAD)),
                  _const_spec((GATE_PAD, 2 * QD)), _const_spec((1, 2 * QD))],
        out_specs=[rows(QD), rows(QD), rows(VD), rows(VD), rows(2 * QD)],
        out_shape=[out(QD, BF16), out(QD, BF16), out(VD, BF16), out(VD, BF16), out(2 * QD, F32)],
        compiler_params=_params(1),
        name="gla_in",
    )(x, pre_g, w_main, w_gate, wa2, ba2)


GLA_GROUP = 8


def _gla_kernel(q_ref, k_ref, v_ref, laf_ref, lab_ref, r_ref, gn_ref, m_ref,
                qfb_ref, dec_ref, kv_ref, st_ref, oacc_ref):
    n_chunks = q_ref.shape[1] // CHUNK
    row = lax.broadcasted_iota(jnp.int32, (CHUNK, CHUNK), 0)
    col = lax.broadcasted_iota(jnp.int32, (CHUNK, CHUNK), 1)
    lower = row >= col
    upper = row <= col
    tri = lower.astype(BF16)
    tri3 = jnp.concatenate([tri, tri, tri], axis=1)
    scale = GLA_DK ** -0.5
    nt = (((1,), (1,)), ((), ()))
    tn = (((0,), (0,)), ((), ()))

    def chunk_rows(n):
        return pl.ds(pl.multiple_of(n * CHUNK, CHUNK), CHUNK)

    def group(i):
        return [i * GLA_GROUP + g for g in range(GLA_GROUP)]

    def front(i, carry):
        chunks = group(i)
        all_rows = [chunk_rows(n) for n in chunks]
        pres = []
        for rows in all_rows:
            la = jnp.concatenate([laf_ref[0, rows, :], lab_ref[0, rows, :]], axis=1)
            hi = la.astype(BF16)
            r1 = la - hi.astype(F32)
            mid = r1.astype(BF16)
            lo = (r1 - mid.astype(F32)).astype(BF16)
            pres.append(_dot(tri3, jnp.concatenate([hi, mid, lo], axis=0)))
        qf, qb, kf, kb, ke = [], [], [], [], []
        for n, rows, pre in zip(chunks, all_rows, pres):
            cum_f = pre[:, :GLA_DK]
            pre_b = pre[:, GLA_DK:]
            tot_f = cum_f[CHUNK - 1:CHUNK, :]
            tot_b = pre_b[CHUNK - 1:CHUNK, :]
            cum_b = tot_b - pre_b + lab_ref[0, rows, :]
            q = q_ref[0, rows, :].astype(F32) * scale
            k = k_ref[0, rows, :].astype(F32)
            kf32 = k * jnp.exp(-cum_f)
            kb32 = k * jnp.exp(-cum_b)
            dec = jnp.concatenate([jnp.exp(tot_f), jnp.exp(tot_b)], axis=1)
            qf.append((q * jnp.exp(cum_f)).astype(BF16))
            qb.append((q * jnp.exp(cum_b)).astype(BF16))
            kf.append(kf32.astype(BF16))
            kb.append(kb32.astype(BF16))
            ke.append(jnp.concatenate([kf32 * dec[:, :GLA_DK], kb32 * dec[:, GLA_DK:]],
                                      axis=1).astype(BF16))
            qfb_ref[rows, :GLA_DK] = qf[-1]
            qfb_ref[rows, GLA_DK:] = qb[-1]
            dec_ref[n] = jnp.broadcast_to(dec, dec_ref.shape[1:])
        s_f = [lax.dot_general(a, b, nt, preferred_element_type=F32) for a, b in zip(qf, kf)]
        s_b = [lax.dot_general(a, b, nt, preferred_element_type=F32) for a, b in zip(qb, kb)]
        for n, rows, k_end in zip(chunks, all_rows, ke):
            kv_ref[n] = lax.dot_general(v_ref[0, rows, :], k_end, tn,
                                        preferred_element_type=F32)
        for rows, sf, sb in zip(all_rows, s_f, s_b):
            scores = jnp.where(lower, sf, 0.0) + jnp.where(upper, sb, 0.0)
            oacc_ref[rows, :] = _dot(scores.astype(BF16), v_ref[0, rows, :])

    def scan(lanes, order):
        def step(i, state):
            n = order(i)
            st_ref[n, :, lanes] = state.astype(BF16)
            return state * dec_ref[n][0:1, lanes] + kv_ref[n][:, lanes]
        lax.fori_loop(0, n_chunks, step, jnp.zeros((GLA_DV, GLA_DK), F32))

    def inter(i, carry):
        for n in group(i):
            rows = chunk_rows(n)
            o = oacc_ref[rows, :] + lax.dot_general(qfb_ref[rows, :], st_ref[n], nt,
                                                    preferred_element_type=F32)
            o = o * lax.rsqrt(jnp.mean(o * o, axis=-1, keepdims=True) + EPS) * gn_ref[...]
            r = r_ref[0, rows, :].astype(F32)
            m_ref[0, rows, :] = (o * (r * jax.nn.sigmoid(r))).astype(BF16)

    def run(body, trips):
        def step(i, carry):
            body(i, carry)
            return carry
        lax.fori_loop(0, trips, step, 0)

    n_groups = n_chunks // GLA_GROUP
    run(front, n_groups)
    scan(slice(0, GLA_DK), lambda i: i)
    scan(slice(GLA_DK, 2 * GLA_DK), lambda i: n_chunks - 1 - i)
    run(inter, n_groups)


def _gla(q, k, v, la, r, gn_g):
    b, s, _ = q.shape
    n_chunks = s // CHUNK

    def head(width, first_block=0):
        return pl.BlockSpec((1, s, width), lambda i, h: (i, 0, first_block + h))

    return pl.pallas_call(
        _gla_kernel,
        grid=(b, GLA_HEADS),
        in_specs=[head(GLA_DK), head(GLA_DK), head(GLA_DV), head(GLA_DK),
                  head(GLA_DK, GLA_HEADS), head(GLA_DV),
                  pl.BlockSpec((1, GLA_DV), lambda i, h: (0, h))],
        out_specs=head(GLA_DV),
        out_shape=jax.ShapeDtypeStruct((b, s, VD), BF16),
        scratch_shapes=[pltpu.VMEM((s, 2 * GLA_DK), BF16),
                        pltpu.VMEM((n_chunks, 8, 2 * GLA_DK), F32),
                        pltpu.VMEM((n_chunks, GLA_DV, 2 * GLA_DK), F32),
                        pltpu.VMEM((n_chunks, GLA_DV, 2 * GLA_DK), BF16),
                        pltpu.VMEM((s, GLA_DV), F32)],
        compiler_params=_params(2),
        name="gla",
    )(q, k, v, la, la, r, gn_g)


def kernel(x, mix_pre_g, mix_post_g, ffn_pre_g, ffn_post_g, cv_w_in, cv_dw_w, cv_dw_b, cv_ln_g,
           cv_ln_b, cv_sc_w, cv_w_out, gla_w_in, gla_wa2_f, gla_ba2_f, gla_wa2_b, gla_ba2_b,
           gla_gn_g, gla_w_out, ffn_w_gu, ffn_w_down):
    b, s, d = x.shape
    t = b * s

    def vec(g):
        return g.reshape(1, -1)

    def mix_out_ffn(xt, m, w_out, layer):
        return _ffn(xt, m, w_out.astype(BF16), vec(mix_post_g[layer]), vec(ffn_pre_g[layer]),
                    ffn_w_gu[layer].astype(BF16), ffn_w_down[layer].astype(BF16),
                    vec(ffn_post_g[layer]))

    xt = x.reshape(t, d)

    a, cv, bg = _conv_in(xt, vec(mix_pre_g[0]), cv_w_in[0].astype(BF16))
    m = _conv_mix(a.reshape(b, s, D_A), cv.reshape(b, s, D_B), bg.reshape(b, s, D_B),
                  cv_dw_w[0], vec(cv_dw_b[0]), vec(cv_ln_g[0]), vec(cv_ln_b[0]), cv_sc_w[0])
    xt = mix_out_ffn(xt, m.reshape(t, d), cv_w_out[0], 0)

    w_in = gla_w_in[0]
    n_main = 2 * QD + 2 * VD
    w_gate = jnp.pad(w_in[:, n_main:], ((0, 0), (0, GATE_PAD - 2 * GATE_RANK))).astype(BF16)
    wa2 = jnp.zeros((GATE_PAD, 2 * QD), F32)
    wa2 = wa2.at[:GATE_RANK, :QD].set(gla_wa2_f[0])
    wa2 = wa2.at[GATE_RANK:2 * GATE_RANK, QD:].set(gla_wa2_b[0])
    ba2 = jnp.concatenate([gla_ba2_f[0], gla_ba2_b[0]]).reshape(1, 2 * QD)
    q, k, v, r, la = _gla_in(xt, vec(mix_pre_g[1]), w_in.astype(BF16), w_gate,
                             wa2.astype(BF16), ba2)
    m = _gla(q.reshape(b, s, QD), k.reshape(b, s, QD), v.reshape(b, s, VD),
             la.reshape(b, s, 2 * QD), r.reshape(b, s, VD), vec(gla_gn_g[0]))
    xt = mix_out_ffn(xt, m.reshape(t, d), gla_w_out[0], 1)
    return xt.reshape(b, s, d)
```

```python
import jax
import jax.numpy as jnp
from jax import lax
from jax.experimental import pallas as pl
from jax.experimental.pallas import tpu as pltpu

D_MODEL = 1024
D_A = 512
D_B = 512
CONV_A_WIDTH = 31
CONV_B_WIDTH = 3
GLA_HEADS = 4
GLA_DK = 128
GLA_DV = 256
GATE_RANK = 16
GATE_TAU = 16.0
CHUNK = 64
D_FF = 2816
EPS = 1e-6

HALO_A = 16
HALO_B = 8
GATE_PAD = 128

PROJ_TM = 1024
PROJ_SUB = 512
TS = 512
VMEM_LIMIT = 56 * 1024 * 1024

F32 = jnp.float32
BF16 = jnp.bfloat16


def _rms(x, g):
    return x * lax.rsqrt(jnp.mean(x * x, axis=-1, keepdims=True) + EPS) * g


def _dot(a, b):
    return jnp.dot(a, b, preferred_element_type=F32)


def _const_spec(shape):
    zeros = (0,) * len(shape)
    return pl.BlockSpec(shape, lambda *_: zeros, pipeline_mode=pl.Buffered(1))


def _params(n_grid):
    return pltpu.CompilerParams(dimension_semantics=("arbitrary",) * n_grid,
                                vmem_limit_bytes=VMEM_LIMIT)


def _sub_tiles(tile, sub):
    return [slice(i * sub, (i + 1) * sub) for i in range(tile // sub)]


FF_CHUNK = 256
FF_TM = 1024
FF_SUB = 512


def _ffn_kernel(x_ref, m_ref, wo_ref, mpost_ref, pre_ref, wgu_ref, wd_ref, post_ref, o_ref,
                act_ref):
    subs = _sub_tiles(FF_TM, FF_SUB)
    mixed = [_dot(m_ref[rows, :], wo_ref[...]) for rows in subs]
    hidden = []
    for rows, mo in zip(subs, mixed):
        x = x_ref[rows, :] + _rms(mo, mpost_ref[...])
        o_ref[rows, :] = x
        hidden.append(_rms(x, pre_ref[...]).astype(BF16))
    for rows, h in zip(subs, hidden):
        for c in range(D_FF // FF_CHUNK):
            lo = c * FF_CHUNK
            g = _dot(h, wgu_ref[:, lo:lo + FF_CHUNK])
            u = _dot(h, wgu_ref[:, D_FF + lo:D_FF + lo + FF_CHUNK])
            act_ref[rows, lo:lo + FF_CHUNK] = (g * jax.nn.sigmoid(g) * u).astype(BF16)
    down = [_dot(act_ref[rows, :], wd_ref[...]) for rows in subs]
    for rows, f in zip(subs, down):
        o_ref[rows, :] = o_ref[rows, :] + _rms(f, post_ref[...])


def _ffn(x, m, w_out, mix_post_g, pre_g, w_gu, w_d, post_g):
    t = x.shape[0]
    row = pl.BlockSpec((FF_TM, D_MODEL), lambda i: (i, 0))
    return pl.pallas_call(
        _ffn_kernel,
        grid=(t // FF_TM,),
        in_specs=[row, row, _const_spec((D_MODEL, D_MODEL)), _const_spec((1, D_MODEL)),
                  _const_spec((1, D_MODEL)), _const_spec((D_MODEL, 2 * D_FF)),
                  _const_spec((D_FF, D_MODEL)), _const_spec((1, D_MODEL))],
        out_specs=row,
        out_shape=jax.ShapeDtypeStruct((t, D_MODEL), F32),
        scratch_shapes=[pltpu.VMEM((FF_TM, D_FF), BF16)],
        compiler_params=_params(1),
        name="ffn",
    )(x, m, w_out, mix_post_g, pre_g, w_gu, w_d, post_g)


def _conv_in_kernel(x_ref, pre_ref, w_ref, a_ref, cv_ref, bg_ref):
    subs = _sub_tiles(PROJ_TM, PROJ_SUB)
    hidden = [_rms(x_ref[rows, :], pre_ref[...]).astype(BF16) for rows in subs]
    for rows, h in zip(subs, hidden):
        def proj(j, h=h):
            return _dot(h, w_ref[:, j * D_A:(j + 1) * D_A])

        a_ref[rows, :] = proj(0) * jax.nn.sigmoid(proj(1))
        bg_ref[rows, :] = proj(2)
        cv_ref[rows, :] = proj(3) * proj(4)


def _conv_in(x, pre_g, w_in):
    t = x.shape[0]
    row = pl.BlockSpec((PROJ_TM, D_MODEL), lambda i: (i, 0))
    half = pl.BlockSpec((PROJ_TM, D_A), lambda i: (i, 0))
    out = jax.ShapeDtypeStruct((t, D_A), F32)
    return pl.pallas_call(
        _conv_in_kernel,
        grid=(t // PROJ_TM,),
        in_specs=[row, _const_spec((1, D_MODEL)), _const_spec((D_MODEL, 5 * D_A))],
        out_specs=[half, half, half],
        out_shape=[out, out, out],
        compiler_params=_params(1),
        name="conv_in",
    )(x, pre_g, w_in)


CONV_ROWS = 256
CONV_LANES = 128
NORM_ROWS = 32
NORM_UNROLL = 8
SUBLANES = 8
PAD_A = (CONV_A_WIDTH - 1) // 2
PAD_B = (CONV_B_WIDTH - 1) // 2
SHIFTS_B = sorted({(HALO_B + k - PAD_B) % SUBLANES for k in range(CONV_B_WIDTH)})


def _conv_mix_kernel(a_ref, ap_ref, an_ref, cv_ref, cp_ref, cn_ref, bg_ref,
                     dww_ref, dwb_ref, lng_ref, lnb_ref, scw_ref, m_ref, ash, csh, conv_ref):
    j = pl.program_id(1)
    first = j == 0
    last = j == pl.num_programs(1) - 1
    ash[0, 0:HALO_A, :] = jnp.where(first, 0.0, ap_ref[0])
    ash[0, HALO_A:HALO_A + TS, :] = a_ref[0]
    ash[0, HALO_A + TS:, :] = jnp.where(last, 0.0, an_ref[0])
    csh[0, 0:HALO_B, :] = jnp.where(first, 0.0, cp_ref[0])
    csh[0, HALO_B:HALO_B + TS, :] = cv_ref[0]
    csh[0, HALO_B + TS:, :] = jnp.where(last, 0.0, cn_ref[0])
    rows_a = TS + 2 * HALO_A - SUBLANES
    for r in range(1, SUBLANES):
        ash[r, 0:rows_a, :] = ash[0, r:r + rows_a, :]
    rows_b = TS + 2 * HALO_B - SUBLANES
    for i, r in enumerate(SHIFTS_B):
        if r:
            csh[i, 0:rows_b, :] = csh[0, r:r + rows_b, :]

    groups = CONV_ROWS // SUBLANES

    for cb in range(D_A // CONV_LANES):
        lanes = slice(cb * CONV_LANES, (cb + 1) * CONV_LANES)

        def conv_block(i, carry, lanes=lanes):
            r0 = pl.multiple_of(i * CONV_ROWS, CONV_ROWS)
            acc = None
            for r in range(SUBLANES):
                taps = [k for k in range(CONV_A_WIDTH) if (HALO_A + k - PAD_A) % SUBLANES == r]
                offs = [HALO_A + k - PAD_A - r for k in taps]
                lo = min(offs)
                window = ash[r, pl.ds(r0 + lo, max(offs) - lo + CONV_ROWS), lanes]
                for k, off in zip(taps, offs):
                    slab = window[off - lo:off - lo + CONV_ROWS]
                    term = slab.reshape(groups, SUBLANES, CONV_LANES) * dww_ref[k, :, lanes][None]
                    acc = term if acc is None else acc + term
            conv_ref[pl.ds(r0, CONV_ROWS), lanes] = acc.reshape(CONV_ROWS, CONV_LANES)
            return carry

        lax.fori_loop(0, TS // CONV_ROWS, conv_block, 0)

    def norm_block(r0):
        rows = pl.ds(r0, NORM_ROWS)
        acc = conv_ref[rows, :] + dwb_ref[...]
        mu = jnp.mean(acc, axis=-1, keepdims=True)
        xc = acc - mu
        y = xc * lax.rsqrt(jnp.mean(xc * xc, axis=-1, keepdims=True) + EPS)
        y = y * lng_ref[...] + lnb_ref[...]
        m_ref[0, rows, 0:D_A] = (y * jax.nn.sigmoid(y)).astype(BF16)

        accb = None
        for k in range(CONV_B_WIDTH):
            off = HALO_B + k - PAD_B
            slab = csh[SHIFTS_B.index(off % SUBLANES), pl.ds(r0 + off - off % SUBLANES, NORM_ROWS), :]
            term = slab.reshape(NORM_ROWS // SUBLANES, SUBLANES, D_B) * scw_ref[k][None]
            accb = term if accb is None else accb + term
        m_ref[0, rows, D_A:] = (bg_ref[0, rows, :] * accb.reshape(NORM_ROWS, D_B)).astype(BF16)

    def norm_step(i, carry):
        for u in range(NORM_UNROLL):
            norm_block(pl.multiple_of((i * NORM_UNROLL + u) * NORM_ROWS, NORM_ROWS))
        return carry

    lax.fori_loop(0, TS // (NORM_ROWS * NORM_UNROLL), norm_step, 0)


def _conv_mix(a, cv, bg, dw_w, dw_b, ln_g, ln_b, sc_w):
    b, s, _ = a.shape
    n_s = s // TS
    grid = (b, n_s)

    def main(width):
        return pl.BlockSpec((1, TS, width), lambda i, j: (i, j, 0))

    def prev(halo, width):
        per = TS // halo
        return pl.BlockSpec((1, halo, width), lambda i, j: (i, jnp.maximum(j * per - 1, 0), 0))

    def nxt(halo, width):
        per = TS // halo
        n_blocks = s // halo
        return pl.BlockSpec((1, halo, width),
                            lambda i, j: (i, jnp.minimum((j + 1) * per, n_blocks - 1), 0))

    return pl.pallas_call(
        _conv_mix_kernel,
        grid=grid,
        in_specs=[main(D_A), prev(HALO_A, D_A), nxt(HALO_A, D_A),
                  main(D_B), prev(HALO_B, D_B), nxt(HALO_B, D_B),
                  main(D_B),
                  _const_spec((CONV_A_WIDTH, SUBLANES, D_A)), _const_spec((1, D_A)),
                  _const_spec((1, D_A)), _const_spec((1, D_A)),
                  _const_spec((CONV_B_WIDTH, SUBLANES, D_B))],
        out_specs=main(D_A + D_B),
        out_shape=jax.ShapeDtypeStruct((b, s, D_A + D_B), BF16),
        scratch_shapes=[pltpu.VMEM((SUBLANES, TS + 2 * HALO_A, D_A), F32),
                        pltpu.VMEM((len(SHIFTS_B), TS + 2 * HALO_B, D_B), F32),
                        pltpu.VMEM((TS, D_A), F32)],
        compiler_params=_params(2),
        name="conv_mix",
    )(a, a, a, cv, cv, cv, bg,
      jnp.broadcast_to(dw_w[:, None, :], (CONV_A_WIDTH, SUBLANES, D_A)), dw_b, ln_g, ln_b,
      jnp.broadcast_to(sc_w[:, None, :], (CONV_B_WIDTH, SUBLANES, D_B)))


QD = GLA_HEADS * GLA_DK
VD = GLA_HEADS * GLA_DV


def _gla_in_kernel(x_ref, pre_ref, w_ref, wgate_ref, wa2_ref, ba2_ref,
                   q_ref, k_ref, v_ref, r_ref, la_ref):
    subs = _sub_tiles(PROJ_TM, PROJ_SUB)
    hidden = [_rms(x_ref[rows, :], pre_ref[...]).astype(BF16) for rows in subs]
    for rows, h in zip(subs, hidden):
        gate = _dot(h, wgate_ref[...]).astype(BF16)
        z = _dot(gate, wa2_ref[...]) + ba2_ref[...]
        log_sig = jnp.minimum(z, 0.0) - jnp.log(1.0 + jnp.exp(-jnp.abs(z)))
        la_ref[rows, :] = log_sig * (1.0 / GATE_TAU)
        q_ref[rows, :] = _dot(h, w_ref[:, 0:QD]).astype(BF16)
        k_ref[rows, :] = _dot(h, w_ref[:, QD:2 * QD]).astype(BF16)
        v_ref[rows, :] = _dot(h, w_ref[:, 2 * QD:2 * QD + VD]).astype(BF16)
        r_ref[rows, :] = _dot(h, w_ref[:, 2 * QD + VD:2 * QD + 2 * VD]).astype(BF16)


def _gla_in(x, pre_g, w_main, w_gate, wa2, ba2):
    t = x.shape[0]

    def rows(width):
        return pl.BlockSpec((PROJ_TM, width), lambda i: (i, 0))

    def out(width, dtype):
        return jax.ShapeDtypeStruct((t, width), dtype)

    return pl.pallas_call(
        _gla_in_kernel,
        grid=(t // PROJ_TM,),
        in_specs=[rows(D_MODEL), _const_spec((1, D_MODEL)),
                  _const_spec(w_main.shape), _const_spec((D_MODEL, GATE_PAD)),
                  _const_spec((GATE_PAD, 2 * QD)), _const_spec((1, 2 * QD))],
        out_specs=[rows(QD), rows(QD), rows(VD), rows(VD), rows(2 * QD)],
        out_shape=[out(QD, BF16), out(QD, BF16), out(VD, BF16), out(VD, BF16), out(2 * QD, F32)],
        compiler_params=_params(1),
        name="gla_in",
    )(x, pre_g, w_main, w_gate, wa2, ba2)


GLA_SUPER = 2 * CHUNK
GLA_GROUP = 16


def _gla_kernel(q_ref, k_ref, v_ref, laf_ref, lab_ref, r_ref, gn_ref, m_ref,
                qfb_ref, dec_ref, kv_ref, st_ref, oacc_ref):
    n_pairs = q_ref.shape[1] // GLA_SUPER
    row = lax.broadcasted_iota(jnp.int32, (CHUNK, GLA_SUPER), 0)
    col = lax.broadcasted_iota(jnp.int32, (CHUNK, GLA_SUPER), 1)
    in_a = col < CHUNK
    mask_a_fwd = in_a & (row >= col)
    mask_a_bwd = (~in_a) | (row <= col)
    mask_b_fwd = in_a | (row >= col - CHUNK)
    mask_b_bwd = (~in_a) & (row <= col - CHUNK)
    tri = (lax.broadcasted_iota(jnp.int32, (CHUNK, CHUNK), 0)
           >= lax.broadcasted_iota(jnp.int32, (CHUNK, CHUNK), 1)).astype(BF16)
    tri3 = jnp.concatenate([tri, tri, tri], axis=1)
    scale = GLA_DK ** -0.5
    nt = (((1,), (1,)), ((), ()))
    tn = (((0,), (0,)), ((), ()))

    def pair_rows(p):
        return pl.ds(pl.multiple_of(p * GLA_SUPER, GLA_SUPER), GLA_SUPER)

    def group(i):
        return [i * GLA_GROUP + g for g in range(GLA_GROUP)]

    def prefix_sums(la):
        hi = la.astype(BF16)
        r1 = la - hi.astype(F32)
        mid = r1.astype(BF16)
        lo = (r1 - mid.astype(F32)).astype(BF16)
        return _dot(tri3, jnp.concatenate([hi, mid, lo], axis=0))

    def decayed(q, k, lab, pre):
        cum_f = pre[:, :GLA_DK]
        pre_b = pre[:, GLA_DK:]
        tot_f = cum_f[CHUNK - 1:CHUNK, :]
        tot_b = pre_b[CHUNK - 1:CHUNK, :]
        cum_b = tot_b - pre_b + lab
        kf = k * jnp.exp(-cum_f)
        kb = k * jnp.exp(-cum_b)
        dec_f = jnp.exp(tot_f)
        dec_b = jnp.exp(tot_b)
        return dict(qf=q * jnp.exp(cum_f), qb=q * jnp.exp(cum_b), kf=kf, kb=kb,
                    kef=kf * dec_f, keb=kb * dec_b, dec_f=dec_f, dec_b=dec_b)

    def rows_cat(top, bottom):
        return jnp.concatenate([top.astype(BF16), bottom.astype(BF16)], axis=0)

    def front(i, carry):
        pairs = group(i)
        all_rows = [pair_rows(p) for p in pairs]
        las, pres = [], []
        for rows in all_rows:
            la = jnp.concatenate([laf_ref[0, rows, :], lab_ref[0, rows, :]], axis=1)
            las.append(la)
            pres.append((prefix_sums(la[:CHUNK]), prefix_sums(la[CHUNK:])))
        score_ops, kv_ops = [], []
        for p, rows, la, (pre_a, pre_b) in zip(pairs, all_rows, las, pres):
            q = q_ref[0, rows, :].astype(F32) * scale
            k = k_ref[0, rows, :].astype(F32)
            ca = decayed(q[:CHUNK], k[:CHUNK], la[:CHUNK, GLA_DK:], pre_a)
            cb = decayed(q[CHUNK:], k[CHUNK:], la[CHUNK:, GLA_DK:], pre_b)
            qfb_ref[rows, :GLA_DK] = rows_cat(ca["qf"], cb["qf"] * ca["dec_f"])
            qfb_ref[rows, GLA_DK:] = rows_cat(ca["qb"] * cb["dec_b"], cb["qb"])
            k_end = jnp.concatenate([rows_cat(ca["kef"] * cb["dec_f"], cb["kef"]),
                                     rows_cat(ca["keb"], cb["keb"] * ca["dec_b"])], axis=1)
            dec = jnp.concatenate([ca["dec_f"] * cb["dec_f"], ca["dec_b"] * cb["dec_b"]], axis=1)
            dec_ref[p] = jnp.broadcast_to(dec, dec_ref.shape[1:])
            kv_ops.append(k_end)
            score_ops.append((ca["qf"].astype(BF16), rows_cat(ca["kf"], cb["kf"]),
                              ca["qb"].astype(BF16), rows_cat(ca["kb"], cb["keb"]),
                              cb["qf"].astype(BF16), rows_cat(ca["kef"], cb["kf"]),
                              cb["qb"].astype(BF16), rows_cat(ca["kb"], cb["kb"])))
        scores = [[lax.dot_general(ops[2 * j], ops[2 * j + 1], nt, preferred_element_type=F32)
                   for j in range(4)] for ops in score_ops]
        for p, rows, k_end in zip(pairs, all_rows, kv_ops):
            kv_ref[p] = lax.dot_general(v_ref[0, rows, :], k_end, tn,
                                        preferred_element_type=F32)
        for rows, (a_f, a_b, b_f, b_b) in zip(all_rows, scores):
            top = jnp.where(mask_a_fwd, a_f, 0.0) + jnp.where(mask_a_bwd, a_b, 0.0)
            bottom = jnp.where(mask_b_fwd, b_f, 0.0) + jnp.where(mask_b_bwd, b_b, 0.0)
            oacc_ref[rows, :] = _dot(rows_cat(top, bottom), v_ref[0, rows, :])

    def scan(lanes, order):
        def step(i, state):
            p = order(i)
            st_ref[p, :, lanes] = state.astype(BF16)
            return state * dec_ref[p][0:1, lanes] + kv_ref[p][:, lanes]
        lax.fori_loop(0, n_pairs, step, jnp.zeros((GLA_DV, GLA_DK), F32))

    def inter(i, carry):
        for p in group(i):
            rows = pair_rows(p)
            o = oacc_ref[rows, :] + lax.dot_general(qfb_ref[rows, :], st_ref[p], nt,
                                                    preferred_element_type=F32)
            o = o * lax.rsqrt(jnp.mean(o * o, axis=-1, keepdims=True) + EPS) * gn_ref[...]
            r = r_ref[0, rows, :].astype(F32)
            m_ref[0, rows, :] = (o * (r * jax.nn.sigmoid(r))).astype(BF16)

    def run(body, trips):
        def step(i, carry):
            body(i, carry)
            return carry
        lax.fori_loop(0, trips, step, 0)

    n_groups = n_pairs // GLA_GROUP
    run(front, n_groups)
    scan(slice(0, GLA_DK), lambda i: i)
    scan(slice(GLA_DK, 2 * GLA_DK), lambda i: n_pairs - 1 - i)
    run(inter, n_groups)


def _gla(q, k, v, la, r, gn_g):
    b, s, _ = q.shape
    n_pairs = s // GLA_SUPER

    def head(width, first_block=0):
        return pl.BlockSpec((1, s, width), lambda i, h: (i, 0, first_block + h))

    return pl.pallas_call(
        _gla_kernel,
        grid=(b, GLA_HEADS),
        in_specs=[head(GLA_DK), head(GLA_DK), head(GLA_DV), head(GLA_DK),
                  head(GLA_DK, GLA_HEADS), head(GLA_DV),
                  pl.BlockSpec((1, GLA_DV), lambda i, h: (0, h))],
        out_specs=head(GLA_DV),
        out_shape=jax.ShapeDtypeStruct((b, s, VD), BF16),
        scratch_shapes=[pltpu.VMEM((s, 2 * GLA_DK), BF16),
                        pltpu.VMEM((n_pairs, 8, 2 * GLA_DK), F32),
                        pltpu.VMEM((n_pairs, GLA_DV, 2 * GLA_DK), F32),
                        pltpu.VMEM((n_pairs, GLA_DV, 2 * GLA_DK), BF16),
                        pltpu.VMEM((s, GLA_DV), F32)],
        compiler_params=_params(2),
        name="gla",
    )(q, k, v, la, la, r, gn_g)


def kernel(x, mix_pre_g, mix_post_g, ffn_pre_g, ffn_post_g, cv_w_in, cv_dw_w, cv_dw_b, cv_ln_g,
           cv_ln_b, cv_sc_w, cv_w_out, gla_w_in, gla_wa2_f, gla_ba2_f, gla_wa2_b, gla_ba2_b,
           gla_gn_g, gla_w_out, ffn_w_gu, ffn_w_down):
    b, s, d = x.shape
    t = b * s

    def vec(g):
        return g.reshape(1, -1)

    def mix_out_ffn(xt, m, w_out, layer):
        return _ffn(xt, m, w_out.astype(BF16), vec(mix_post_g[layer]), vec(ffn_pre_g[layer]),
                    ffn_w_gu[layer].astype(BF16), ffn_w_down[layer].astype(BF16),
                    vec(ffn_post_g[layer]))

    xt = x.reshape(t, d)

    a, cv, bg = _conv_in(xt, vec(mix_pre_g[0]), cv_w_in[0].astype(BF16))
    m = _conv_mix(a.reshape(b, s, D_A), cv.reshape(b, s, D_B), bg.reshape(b, s, D_B),
                  cv_dw_w[0], vec(cv_dw_b[0]), vec(cv_ln_g[0]), vec(cv_ln_b[0]), cv_sc_w[0])
    xt = mix_out_ffn(xt, m.reshape(t, d), cv_w_out[0], 0)

    w_in = gla_w_in[0]
    n_main = 2 * QD + 2 * VD
    w_gate = jnp.pad(w_in[:, n_main:], ((0, 0), (0, GATE_PAD - 2 * GATE_RANK))).astype(BF16)
    wa2 = jnp.zeros((GATE_PAD, 2 * QD), F32)
    wa2 = wa2.at[:GATE_RANK, :QD].set(gla_wa2_f[0])
    wa2 = wa2.at[GATE_RANK:2 * GATE_RANK, QD:].set(gla_wa2_b[0])
    ba2 = jnp.concatenate([gla_ba2_f[0], gla_ba2_b[0]]).reshape(1, 2 * QD)
    q, k, v, r, la = _gla_in(xt, vec(mix_pre_g[1]), w_in.astype(BF16), w_gate,
                             wa2.astype(BF16), ba2)
    m = _gla(q.reshape(b, s, QD), k.reshape(b, s, QD), v.reshape(b, s, VD),
             la.reshape(b, s, 2 * QD), r.reshape(b, s, VD), vec(gla_gn_g[0]))
    xt = mix_out_ffn(xt, m.reshape(t, d), gla_w_out[0], 1)
    return xt.reshape(b, s, d)
```

```python
import functools

import jax
import jax.numpy as jnp
from jax import lax
from jax.experimental import pallas as pl
from jax.experimental.pallas import tpu as pltpu

D_MODEL = 1024
D_A = 512
D_B = 512
CONV_A_WIDTH = 31
CONV_B_WIDTH = 3
GLA_HEADS = 4
GLA_DK = 128
GLA_DV = 256
GATE_RANK = 16
GATE_TAU = 16.0
CHUNK = 64
D_FF = 2816
EPS = 1e-6

HALO_A = 16
HALO_B = 8
GATE_PAD = 128

PROJ_TM = 1024
PROJ_SUB = 512
TS = 512
VMEM_LIMIT = 56 * 1024 * 1024

F32 = jnp.float32
BF16 = jnp.bfloat16


def _rms(x, g):
    return x * lax.rsqrt(jnp.mean(x * x, axis=-1, keepdims=True) + EPS) * g


def _dot(a, b):
    return jnp.dot(a, b, preferred_element_type=F32)


def _const_spec(shape):
    zeros = (0,) * len(shape)
    return pl.BlockSpec(shape, lambda *_: zeros, pipeline_mode=pl.Buffered(1))


def _params(n_grid):
    return pltpu.CompilerParams(dimension_semantics=("arbitrary",) * n_grid,
                                vmem_limit_bytes=VMEM_LIMIT)


def _sub_tiles(tile, sub):
    return [slice(i * sub, (i + 1) * sub) for i in range(tile // sub)]


FF_CHUNK = 256
FF_TM = 1024
FF_SUB = 512
FF_STAGE = 256
FF_VMEM_LIMIT = 60 * 1024 * 1024


def _stream_to_bf16(sources, stores, stage, sem):
    copies = [pltpu.make_async_copy(src, stage.at[i % 2], sem.at[i % 2])
              for i, src in enumerate(sources)]
    copies[0].start()
    for i, (copy, store) in enumerate(zip(copies, stores)):
        if i + 1 < len(copies):
            copies[i + 1].start()
        copy.wait()
        store(stage[i % 2].astype(BF16))


def _ffn_kernel(x_ref, m_ref, wo_ref, mpost_ref, pre_ref, wgu_hbm, wd_hbm, post_ref, o_ref,
                act_ref, wgu_ref, wd_ref, stage_gu, stage_d, sem_gu, sem_d, *, layer):
    @pl.when(pl.program_id(0) == 0)
    def _():
        def store_gu(c):
            def store(block):
                wgu_ref[:, c * FF_STAGE:(c + 1) * FF_STAGE] = block
            return store

        def store_d(c):
            def store(block):
                wd_ref[c * FF_STAGE:(c + 1) * FF_STAGE, :] = block
            return store

        n_gu = 2 * D_FF // FF_STAGE
        n_d = D_FF // FF_STAGE
        _stream_to_bf16([wgu_hbm.at[layer, :, pl.ds(c * FF_STAGE, FF_STAGE)] for c in range(n_gu)],
                        [store_gu(c) for c in range(n_gu)], stage_gu, sem_gu)
        _stream_to_bf16([wd_hbm.at[layer, pl.ds(c * FF_STAGE, FF_STAGE), :] for c in range(n_d)],
                        [store_d(c) for c in range(n_d)], stage_d, sem_d)

    subs = _sub_tiles(FF_TM, FF_SUB)
    mixed = [_dot(m_ref[rows, :], wo_ref[...]) for rows in subs]
    hidden = []
    for rows, mo in zip(subs, mixed):
        x = x_ref[rows, :] + _rms(mo, mpost_ref[...])
        o_ref[rows, :] = x
        hidden.append(_rms(x, pre_ref[...]).astype(BF16))
    for rows, h in zip(subs, hidden):
        for c in range(D_FF // FF_CHUNK):
            lo = c * FF_CHUNK
            g = _dot(h, wgu_ref[:, lo:lo + FF_CHUNK])
            u = _dot(h, wgu_ref[:, D_FF + lo:D_FF + lo + FF_CHUNK])
            act_ref[rows, lo:lo + FF_CHUNK] = (g * jax.nn.sigmoid(g) * u).astype(BF16)
    down = [_dot(act_ref[rows, :], wd_ref[...]) for rows in subs]
    for rows, f in zip(subs, down):
        o_ref[rows, :] = o_ref[rows, :] + _rms(f, post_ref[...])


def _ffn(x, m, w_out, mix_post_g, pre_g, w_gu_all, w_d_all, post_g, layer):
    t = x.shape[0]
    row = pl.BlockSpec((FF_TM, D_MODEL), lambda i: (i, 0))
    hbm = pl.BlockSpec(memory_space=pl.ANY)
    return pl.pallas_call(
        functools.partial(_ffn_kernel, layer=layer),
        grid=(t // FF_TM,),
        in_specs=[row, row, _const_spec((D_MODEL, D_MODEL)), _const_spec((1, D_MODEL)),
                  _const_spec((1, D_MODEL)), hbm, hbm, _const_spec((1, D_MODEL))],
        out_specs=row,
        out_shape=jax.ShapeDtypeStruct((t, D_MODEL), F32),
        scratch_shapes=[pltpu.VMEM((FF_TM, D_FF), BF16),
                        pltpu.VMEM((D_MODEL, 2 * D_FF), BF16),
                        pltpu.VMEM((D_FF, D_MODEL), BF16),
                        pltpu.VMEM((2, D_MODEL, FF_STAGE), F32),
                        pltpu.VMEM((2, FF_STAGE, D_MODEL), F32),
                        pltpu.SemaphoreType.DMA((2,)),
                        pltpu.SemaphoreType.DMA((2,))],
        compiler_params=pltpu.CompilerParams(dimension_semantics=("arbitrary",),
                                             vmem_limit_bytes=FF_VMEM_LIMIT),
        name="ffn",
    )(x, m, w_out, mix_post_g, pre_g, w_gu_all, w_d_all, post_g)


def _conv_in_kernel(x_ref, pre_ref, w_ref, a_ref, cv_ref, bg_ref):
    subs = _sub_tiles(PROJ_TM, PROJ_SUB)
    hidden = [_rms(x_ref[rows, :], pre_ref[...]).astype(BF16) for rows in subs]
    for rows, h in zip(subs, hidden):
        def proj(j, h=h):
            return _dot(h, w_ref[:, j * D_A:(j + 1) * D_A])

        a_ref[rows, :] = proj(0) * jax.nn.sigmoid(proj(1))
        bg_ref[rows, :] = proj(2)
        cv_ref[rows, :] = proj(3) * proj(4)


def _conv_in(x, pre_g, w_in):
    t = x.shape[0]
    row = pl.BlockSpec((PROJ_TM, D_MODEL), lambda i: (i, 0))
    half = pl.BlockSpec((PROJ_TM, D_A), lambda i: (i, 0))
    out = jax.ShapeDtypeStruct((t, D_A), F32)
    return pl.pallas_call(
        _conv_in_kernel,
        grid=(t // PROJ_TM,),
        in_specs=[row, _const_spec((1, D_MODEL)), _const_spec((D_MODEL, 5 * D_A))],
        out_specs=[half, half, half],
        out_shape=[out, out, out],
        compiler_params=_params(1),
        name="conv_in",
    )(x, pre_g, w_in)


CONV_ROWS = 256
CONV_LANES = 128
NORM_ROWS = 32
NORM_UNROLL = 8
SUBLANES = 8
PAD_A = (CONV_A_WIDTH - 1) // 2
PAD_B = (CONV_B_WIDTH - 1) // 2
SHIFTS_B = sorted({(HALO_B + k - PAD_B) % SUBLANES for k in range(CONV_B_WIDTH)})


def _conv_mix_kernel(a_ref, ap_ref, an_ref, cv_ref, cp_ref, cn_ref, bg_ref,
                     dww_ref, dwb_ref, lng_ref, lnb_ref, scw_ref, m_ref, ash, csh, conv_ref):
    j = pl.program_id(1)
    first = j == 0
    last = j == pl.num_programs(1) - 1
    ash[0, 0:HALO_A, :] = jnp.where(first, 0.0, ap_ref[0])
    ash[0, HALO_A:HALO_A + TS, :] = a_ref[0]
    ash[0, HALO_A + TS:, :] = jnp.where(last, 0.0, an_ref[0])
    csh[0, 0:HALO_B, :] = jnp.where(first, 0.0, cp_ref[0])
    csh[0, HALO_B:HALO_B + TS, :] = cv_ref[0]
    csh[0, HALO_B + TS:, :] = jnp.where(last, 0.0, cn_ref[0])
    rows_a = TS + 2 * HALO_A - SUBLANES
    for r in range(1, SUBLANES):
        ash[r, 0:rows_a, :] = ash[0, r:r + rows_a, :]
    rows_b = TS + 2 * HALO_B - SUBLANES
    for i, r in enumerate(SHIFTS_B):
        if r:
            csh[i, 0:rows_b, :] = csh[0, r:r + rows_b, :]

    groups = CONV_ROWS // SUBLANES

    for cb in range(D_A // CONV_LANES):
        lanes = slice(cb * CONV_LANES, (cb + 1) * CONV_LANES)

        def conv_block(i, carry, lanes=lanes):
            r0 = pl.multiple_of(i * CONV_ROWS, CONV_ROWS)
            acc = None
            for r in range(SUBLANES):
                taps = [k for k in range(CONV_A_WIDTH) if (HALO_A + k - PAD_A) % SUBLANES == r]
                offs = [HALO_A + k - PAD_A - r for k in taps]
                lo = min(offs)
                window = ash[r, pl.ds(r0 + lo, max(offs) - lo + CONV_ROWS), lanes]
                for k, off in zip(taps, offs):
                    slab = window[off - lo:off - lo + CONV_ROWS]
                    term = slab.reshape(groups, SUBLANES, CONV_LANES) * dww_ref[k, :, lanes][None]
                    acc = term if acc is None else acc + term
            conv_ref[pl.ds(r0, CONV_ROWS), lanes] = acc.reshape(CONV_ROWS, CONV_LANES)
            return carry

        lax.fori_loop(0, TS // CONV_ROWS, conv_block, 0)

    def norm_block(r0):
        rows = pl.ds(r0, NORM_ROWS)
        acc = conv_ref[rows, :] + dwb_ref[...]
        mu = jnp.mean(acc, axis=-1, keepdims=True)
        xc = acc - mu
        y = xc * lax.rsqrt(jnp.mean(xc * xc, axis=-1, keepdims=True) + EPS)
        y = y * lng_ref[...] + lnb_ref[...]
        m_ref[0, rows, 0:D_A] = (y * jax.nn.sigmoid(y)).astype(BF16)

        accb = None
        for k in range(CONV_B_WIDTH):
            off = HALO_B + k - PAD_B
            slab = csh[SHIFTS_B.index(off % SUBLANES), pl.ds(r0 + off - off % SUBLANES, NORM_ROWS), :]
            term = slab.reshape(NORM_ROWS // SUBLANES, SUBLANES, D_B) * scw_ref[k][None]
            accb = term if accb is None else accb + term
        m_ref[0, rows, D_A:] = (bg_ref[0, rows, :] * accb.reshape(NORM_ROWS, D_B)).astype(BF16)

    def norm_step(i, carry):
        for u in range(NORM_UNROLL):
            norm_block(pl.multiple_of((i * NORM_UNROLL + u) * NORM_ROWS, NORM_ROWS))
        return carry

    lax.fori_loop(0, TS // (NORM_ROWS * NORM_UNROLL), norm_step, 0)


def _conv_mix(a, cv, bg, dw_w, dw_b, ln_g, ln_b, sc_w):
    b, s, _ = a.shape
    n_s = s // TS
    grid = (b, n_s)

    def main(width):
        return pl.BlockSpec((1, TS, width), lambda i, j: (i, j, 0))

    def prev(halo, width):
        per = TS // halo
        return pl.BlockSpec((1, halo, width), lambda i, j: (i, jnp.maximum(j * per - 1, 0), 0))

    def nxt(halo, width):
        per = TS // halo
        n_blocks = s // halo
        return pl.BlockSpec((1, halo, width),
                            lambda i, j: (i, jnp.minimum((j + 1) * per, n_blocks - 1), 0))

    return pl.pallas_call(
        _conv_mix_kernel,
        grid=grid,
        in_specs=[main(D_A), prev(HALO_A, D_A), nxt(HALO_A, D_A),
                  main(D_B), prev(HALO_B, D_B), nxt(HALO_B, D_B),
                  main(D_B),
                  _const_spec((CONV_A_WIDTH, SUBLANES, D_A)), _const_spec((1, D_A)),
                  _const_spec((1, D_A)), _const_spec((1, D_A)),
                  _const_spec((CONV_B_WIDTH, SUBLANES, D_B))],
        out_specs=main(D_A + D_B),
        out_shape=jax.ShapeDtypeStruct((b, s, D_A + D_B), BF16),
        scratch_shapes=[pltpu.VMEM((SUBLANES, TS + 2 * HALO_A, D_A), F32),
                        pltpu.VMEM((len(SHIFTS_B), TS + 2 * HALO_B, D_B), F32),
                        pltpu.VMEM((TS, D_A), F32)],
        compiler_params=_params(2),
        name="conv_mix",
    )(a, a, a, cv, cv, cv, bg,
      jnp.broadcast_to(dw_w[:, None, :], (CONV_A_WIDTH, SUBLANES, D_A)), dw_b, ln_g, ln_b,
      jnp.broadcast_to(sc_w[:, None, :], (CONV_B_WIDTH, SUBLANES, D_B)))


QD = GLA_HEADS * GLA_DK
VD = GLA_HEADS * GLA_DV


def _gla_in_kernel(x_ref, pre_ref, w_ref, wgate_ref, wa2_ref, ba2_ref,
                   q_ref, k_ref, v_ref, r_ref, la_ref):
    subs = _sub_tiles(PROJ_TM, PROJ_SUB)
    hidden = [_rms(x_ref[rows, :], pre_ref[...]).astype(BF16) for rows in subs]
    for rows, h in zip(subs, hidden):
        gate = _dot(h, wgate_ref[...]).astype(BF16)
        q_ref[rows, :] = _dot(h, w_ref[:, 0:QD]).astype(BF16)
        z = _dot(gate, wa2_ref[...]) + ba2_ref[...]
        k_ref[rows, :] = _dot(h, w_ref[:, QD:2 * QD]).astype(BF16)
        log_sig = jnp.minimum(z, 0.0) - jnp.log(1.0 + jnp.exp(-jnp.abs(z)))
        la_ref[rows, :] = log_sig * (1.0 / GATE_TAU)
        v_ref[rows, :] = _dot(h, w_ref[:, 2 * QD:2 * QD + VD]).astype(BF16)
        r_ref[rows, :] = _dot(h, w_ref[:, 2 * QD + VD:2 * QD + 2 * VD]).astype(BF16)


def _gla_in(x, pre_g, w_main, w_gate, wa2, ba2):
    t = x.shape[0]

    def rows(width):
        return pl.BlockSpec((PROJ_TM, width), lambda i: (i, 0))

    def out(width, dtype):
        return jax.ShapeDtypeStruct((t, width), dtype)

    return pl.pallas_call(
        _gla_in_kernel,
        grid=(t // PROJ_TM,),
        in_specs=[rows(D_MODEL), _const_spec((1, D_MODEL)),
                  _const_spec(w_main.shape), _const_spec((D_MODEL, GATE_PAD)),
                  _const_spec((GATE_PAD, 2 * QD)), _const_spec((1, 2 * QD))],
        out_specs=[rows(QD), rows(QD), rows(VD), rows(VD), rows(2 * QD)],
        out_shape=[out(QD, BF16), out(QD, BF16), out(VD, BF16), out(VD, BF16), out(2 * QD, F32)],
        compiler_params=_params(1),
        name="gla_in",
    )(x, pre_g, w_main, w_gate, wa2, ba2)


GLA_SUPER = 2 * CHUNK
GLA_GROUP = 16


def _gla_kernel(q_ref, k_ref, v_ref, laf_ref, lab_ref, r_ref, gn_ref, m_ref,
                qfb_ref, dec_ref, kv_ref, st_ref, oacc_ref):
    n_pairs = q_ref.shape[1] // GLA_SUPER
    row = lax.broadcasted_iota(jnp.int32, (CHUNK, GLA_SUPER), 0)
    col = lax.broadcasted_iota(jnp.int32, (CHUNK, GLA_SUPER), 1)
    in_a = col < CHUNK
    mask_a_fwd = in_a & (row >= col)
    mask_a_bwd = (~in_a) | (row <= col)
    mask_b_fwd = in_a | (row >= col - CHUNK)
    mask_b_bwd = (~in_a) & (row <= col - CHUNK)
    tri = (lax.broadcasted_iota(jnp.int32, (CHUNK, CHUNK), 0)
           >= lax.broadcasted_iota(jnp.int32, (CHUNK, CHUNK), 1)).astype(BF16)
    tri3 = jnp.concatenate([tri, tri, tri], axis=1)
    scale = GLA_DK ** -0.5
    nt = (((1,), (1,)), ((), ()))
    tn = (((0,), (0,)), ((), ()))

    def pair_rows(p):
        return pl.ds(pl.multiple_of(p * GLA_SUPER, GLA_SUPER), GLA_SUPER)

    def group(i):
        return [i * GLA_GROUP + g for g in range(GLA_GROUP)]

    def prefix_sums(la):
        hi = la.astype(BF16)
        r1 = la - hi.astype(F32)
        mid = r1.astype(BF16)
        lo = (r1 - mid.astype(F32)).astype(BF16)
        return _dot(tri3, jnp.concatenate([hi, mid, lo], axis=0))

    def decayed(q, k, lab, pre):
        cum_f = pre[:, :GLA_DK]
        pre_b = pre[:, GLA_DK:]
        tot_f = cum_f[CHUNK - 1:CHUNK, :]
        tot_b = pre_b[CHUNK - 1:CHUNK, :]
        cum_b = tot_b - pre_b + lab
        kf = k * jnp.exp(-cum_f)
        kb = k * jnp.exp(-cum_b)
        dec_f = jnp.exp(tot_f)
        dec_b = jnp.exp(tot_b)
        return dict(qf=q * jnp.exp(cum_f), qb=q * jnp.exp(cum_b), kf=kf, kb=kb,
                    kef=kf * dec_f, keb=kb * dec_b, dec_f=dec_f, dec_b=dec_b)

    def rows_cat(top, bottom):
        return jnp.concatenate([top.astype(BF16), bottom.astype(BF16)], axis=0)

    def front(i, carry):
        pairs = group(i)
        all_rows = [pair_rows(p) for p in pairs]
        las, pres = [], []
        for rows in all_rows:
            la = jnp.concatenate([laf_ref[0, rows, :], lab_ref[0, rows, :]], axis=1)
            las.append(la)
            pres.append((prefix_sums(la[:CHUNK]), prefix_sums(la[CHUNK:])))
        score_ops, kv_ops = [], []
        for p, rows, la, (pre_a, pre_b) in zip(pairs, all_rows, las, pres):
            q = q_ref[0, rows, :].astype(F32) * scale
            k = k_ref[0, rows, :].astype(F32)
            ca = decayed(q[:CHUNK], k[:CHUNK], la[:CHUNK, GLA_DK:], pre_a)
            cb = decayed(q[CHUNK:], k[CHUNK:], la[CHUNK:, GLA_DK:], pre_b)
            qfb_ref[rows, :GLA_DK] = rows_cat(ca["qf"], cb["qf"] * ca["dec_f"])
            qfb_ref[rows, GLA_DK:] = rows_cat(ca["qb"] * cb["dec_b"], cb["qb"])
            k_end = jnp.concatenate([rows_cat(ca["kef"] * cb["dec_f"], cb["kef"]),
                                     rows_cat(ca["keb"], cb["keb"] * ca["dec_b"])], axis=1)
            dec = jnp.concatenate([ca["dec_f"] * cb["dec_f"], ca["dec_b"] * cb["dec_b"]], axis=1)
            dec_ref[p] = jnp.broadcast_to(dec, dec_ref.shape[1:])
            kv_ops.append(k_end)
            score_ops.append((ca["qf"].astype(BF16), rows_cat(ca["kf"], cb["kf"]),
                              ca["qb"].astype(BF16), rows_cat(ca["kb"], cb["keb"]),
                              cb["qf"].astype(BF16), rows_cat(ca["kef"], cb["kf"]),
                              cb["qb"].astype(BF16), rows_cat(ca["kb"], cb["kb"])))
        scores = [[lax.dot_general(ops[2 * j], ops[2 * j + 1], nt, preferred_element_type=F32)
                   for j in range(4)] for ops in score_ops]
        for p, rows, k_end in zip(pairs, all_rows, kv_ops):
            kv_ref[p] = lax.dot_general(v_ref[0, rows, :], k_end, tn,
                                        preferred_element_type=F32)
        for rows, (a_f, a_b, b_f, b_b) in zip(all_rows, scores):
            top = jnp.where(mask_a_fwd, a_f, 0.0) + jnp.where(mask_a_bwd, a_b, 0.0)
            bottom = jnp.where(mask_b_fwd, b_f, 0.0) + jnp.where(mask_b_bwd, b_b, 0.0)
            oacc_ref[rows, :] = _dot(rows_cat(top, bottom), v_ref[0, rows, :])

    def scan(lanes, order):
        def step(i, state):
            p = order(i)
            st_ref[p, :, lanes] = state.astype(BF16)
            return state * dec_ref[p][0:1, lanes] + kv_ref[p][:, lanes]
        lax.fori_loop(0, n_pairs, step, jnp.zeros((GLA_DV, GLA_DK), F32))

    def inter(i, carry):
        for p in group(i):
            rows = pair_rows(p)
            o = oacc_ref[rows, :] + lax.dot_general(qfb_ref[rows, :], st_ref[p], nt,
                                                    preferred_element_type=F32)
            o = o * lax.rsqrt(jnp.mean(o * o, axis=-1, keepdims=True) + EPS) * gn_ref[...]
            r = r_ref[0, rows, :].astype(F32)
            m_ref[0, rows, :] = (o * (r * jax.nn.sigmoid(r))).astype(BF16)

    def run(body, trips):
        def step(i, carry):
            body(i, carry)
            return carry
        lax.fori_loop(0, trips, step, 0)

    n_groups = n_pairs // GLA_GROUP
    run(front, n_groups)
    scan(slice(0, GLA_DK), lambda i: i)
    scan(slice(GLA_DK, 2 * GLA_DK), lambda i: n_pairs - 1 - i)
    run(inter, n_groups)


def _gla(q, k, v, la, r, gn_g):
    b, s, _ = q.shape
    n_pairs = s // GLA_SUPER

    def head(width, first_block=0):
        return pl.BlockSpec((1, s, width), lambda i, h: (i, 0, first_block + h))

    return pl.pallas_call(
        _gla_kernel,
        grid=(b, GLA_HEADS),
        in_specs=[head(GLA_DK), head(GLA_DK), head(GLA_DV), head(GLA_DK),
                  head(GLA_DK, GLA_HEADS), head(GLA_DV),
                  pl.BlockSpec((1, GLA_DV), lambda i, h: (0, h))],
        out_specs=head(GLA_DV),
        out_shape=jax.ShapeDtypeStruct((b, s, VD), BF16),
        scratch_shapes=[pltpu.VMEM((s, 2 * GLA_DK), BF16),
                        pltpu.VMEM((n_pairs, 8, 2 * GLA_DK), F32),
                        pltpu.VMEM((n_pairs, GLA_DV, 2 * GLA_DK), F32),
                        pltpu.VMEM((n_pairs, GLA_DV, 2 * GLA_DK), BF16),
                        pltpu.VMEM((s, GLA_DV), F32)],
        compiler_params=_params(2),
        name="gla",
    )(q, k, v, la, la, r, gn_g)


def kernel(x, mix_pre_g, mix_post_g, ffn_pre_g, ffn_post_g, cv_w_in, cv_dw_w, cv_dw_b, cv_ln_g,
           cv_ln_b, cv_sc_w, cv_w_out, gla_w_in, gla_wa2_f, gla_ba2_f, gla_wa2_b, gla_ba2_b,
           gla_gn_g, gla_w_out, ffn_w_gu, ffn_w_down):
    b, s, d = x.shape
    t = b * s

    def vec(g):
        return g.reshape(1, -1)

    def mix_out_ffn(xt, m, w_out, layer):
        return _ffn(xt, m, w_out.astype(BF16), vec(mix_post_g[layer]), vec(ffn_pre_g[layer]),
                    ffn_w_gu, ffn_w_down, vec(ffn_post_g[layer]), layer)

    xt = x.reshape(t, d)

    a, cv, bg = _conv_in(xt, vec(mix_pre_g[0]), cv_w_in[0].astype(BF16))
    m = _conv_mix(a.reshape(b, s, D_A), cv.reshape(b, s, D_B), bg.reshape(b, s, D_B),
                  cv_dw_w[0], vec(cv_dw_b[0]), vec(cv_ln_g[0]), vec(cv_ln_b[0]), cv_sc_w[0])
    xt = mix_out_ffn(xt, m.reshape(t, d), cv_w_out[0], 0)

    w_in = gla_w_in[0]
    n_main = 2 * QD + 2 * VD
    w_gate = jnp.pad(w_in[:, n_main:], ((0, 0), (0, GATE_PAD - 2 * GATE_RANK))).astype(BF16)
    wa2 = jnp.zeros((GATE_PAD, 2 * QD), F32)
    wa2 = wa2.at[:GATE_RANK, :QD].set(gla_wa2_f[0])
    wa2 = wa2.at[GATE_RANK:2 * GATE_RANK, QD:].set(gla_wa2_b[0])
    ba2 = jnp.concatenate([gla_ba2_f[0], gla_ba2_b[0]]).reshape(1, 2 * QD)
    q, k, v, r, la = _gla_in(xt, vec(mix_pre_g[1]), w_in.astype(BF16), w_gate,
                             wa2.astype(BF16), ba2)
    m = _gla(q.reshape(b, s, QD), k.reshape(b, s, QD), v.reshape(b, s, VD),
             la.reshape(b, s, 2 * QD), r.reshape(b, s, VD), vec(gla_gn_g[0]))
    xt = mix_out_ffn(xt, m.reshape(t, d), gla_w_out[0], 1)
    return xt.reshape(b, s, d)
```

```python
import functools

import jax
import jax.numpy as jnp
from jax import lax
from jax.experimental import pallas as pl
from jax.experimental.pallas import tpu as pltpu

D_MODEL = 1024
D_A = 512
D_B = 512
CONV_A_WIDTH = 31
CONV_B_WIDTH = 3
GLA_HEADS = 4
GLA_DK = 128
GLA_DV = 256
GATE_RANK = 16
GATE_TAU = 16.0
CHUNK = 64
D_FF = 2816
EPS = 1e-6

HALO_A = 16
HALO_B = 8
GATE_PAD = 128

PROJ_TM = 1024
PROJ_SUB = 512
TS = 1024
VMEM_LIMIT = 56 * 1024 * 1024

F32 = jnp.float32
BF16 = jnp.bfloat16


def _rms(x, g):
    return x * lax.rsqrt(jnp.mean(x * x, axis=-1, keepdims=True) + EPS) * g


def _dot(a, b):
    return jnp.dot(a, b, preferred_element_type=F32)


def _const_spec(shape):
    zeros = (0,) * len(shape)
    return pl.BlockSpec(shape, lambda *_: zeros, pipeline_mode=pl.Buffered(1))


def _params(n_grid):
    return pltpu.CompilerParams(dimension_semantics=("arbitrary",) * n_grid,
                                vmem_limit_bytes=VMEM_LIMIT)


def _sub_tiles(tile, sub):
    return [slice(i * sub, (i + 1) * sub) for i in range(tile // sub)]


FF_CHUNK = 256
FF_TM = 1024
FF_SUB = 512
FF_GU_ROWS = 64
FF_D_ROWS = 256
FF_VMEM_LIMIT = 60 * 1024 * 1024


def _stream_to_bf16(sources, stores, stage, sem):
    copies = [pltpu.make_async_copy(src, stage.at[i % 2], sem.at[i % 2])
              for i, src in enumerate(sources)]
    copies[0].start()
    for i, (copy, store) in enumerate(zip(copies, stores)):
        if i + 1 < len(copies):
            copies[i + 1].start()
        copy.wait()
        store(stage[i % 2].astype(BF16))


def _ffn_kernel(x_ref, m_ref, wo_ref, mpost_ref, pre_ref, wgu_hbm, wd_hbm, post_ref, o_ref,
                act_ref, wgu_ref, wd_ref, stage_gu, stage_d, sem_gu, sem_d, *, layer):
    @pl.when(pl.program_id(0) == 0)
    def _():
        def row_store(dst, rows, c):
            def store(block):
                dst[c * rows:(c + 1) * rows, :] = block
            return store

        n_gu = D_MODEL // FF_GU_ROWS
        n_d = D_FF // FF_D_ROWS
        _stream_to_bf16([wgu_hbm.at[layer, pl.ds(c * FF_GU_ROWS, FF_GU_ROWS), :] for c in range(n_gu)],
                        [row_store(wgu_ref, FF_GU_ROWS, c) for c in range(n_gu)], stage_gu, sem_gu)
        _stream_to_bf16([wd_hbm.at[layer, pl.ds(c * FF_D_ROWS, FF_D_ROWS), :] for c in range(n_d)],
                        [row_store(wd_ref, FF_D_ROWS, c) for c in range(n_d)], stage_d, sem_d)

    subs = _sub_tiles(FF_TM, FF_SUB)
    mixed = [_dot(m_ref[rows, :], wo_ref[...]) for rows in subs]
    hidden = []
    for rows, mo in zip(subs, mixed):
        x = x_ref[rows, :] + _rms(mo, mpost_ref[...])
        o_ref[rows, :] = x
        hidden.append(_rms(x, pre_ref[...]).astype(BF16))
    for rows, h in zip(subs, hidden):
        for c in range(D_FF // FF_CHUNK):
            lo = c * FF_CHUNK
            g = _dot(h, wgu_ref[:, lo:lo + FF_CHUNK])
            u = _dot(h, wgu_ref[:, D_FF + lo:D_FF + lo + FF_CHUNK])
            act_ref[rows, lo:lo + FF_CHUNK] = (g * jax.nn.sigmoid(g) * u).astype(BF16)
    down = [_dot(act_ref[rows, :], wd_ref[...]) for rows in subs]
    for rows, f in zip(subs, down):
        o_ref[rows, :] = o_ref[rows, :] + _rms(f, post_ref[...])


def _ffn(x, m, w_out, mix_post_g, pre_g, w_gu_all, w_d_all, post_g, layer):
    t = x.shape[0]
    row = pl.BlockSpec((FF_TM, D_MODEL), lambda i: (i, 0))
    hbm = pl.BlockSpec(memory_space=pl.ANY)
    return pl.pallas_call(
        functools.partial(_ffn_kernel, layer=layer),
        grid=(t // FF_TM,),
        in_specs=[row, row, _const_spec((D_MODEL, D_MODEL)), _const_spec((1, D_MODEL)),
                  _const_spec((1, D_MODEL)), hbm, hbm, _const_spec((1, D_MODEL))],
        out_specs=row,
        out_shape=jax.ShapeDtypeStruct((t, D_MODEL), F32),
        scratch_shapes=[pltpu.VMEM((FF_TM, D_FF), BF16),
                        pltpu.VMEM((D_MODEL, 2 * D_FF), BF16),
                        pltpu.VMEM((D_FF, D_MODEL), BF16),
                        pltpu.VMEM((2, FF_GU_ROWS, 2 * D_FF), F32),
                        pltpu.VMEM((2, FF_D_ROWS, D_MODEL), F32),
                        pltpu.SemaphoreType.DMA((2,)),
                        pltpu.SemaphoreType.DMA((2,))],
        compiler_params=pltpu.CompilerParams(dimension_semantics=("arbitrary",),
                                             vmem_limit_bytes=FF_VMEM_LIMIT),
        name="ffn",
    )(x, m, w_out, mix_post_g, pre_g, w_gu_all, w_d_all, post_g)


def _conv_in_kernel(x_ref, pre_ref, w_ref, a_ref, cv_ref, bg_ref):
    subs = _sub_tiles(PROJ_TM, PROJ_SUB)
    hidden = [_rms(x_ref[rows, :], pre_ref[...]).astype(BF16) for rows in subs]
    for rows, h in zip(subs, hidden):
        def proj(j, h=h):
            return _dot(h, w_ref[:, j * D_A:(j + 1) * D_A])

        a_ref[rows, :] = proj(0) * jax.nn.sigmoid(proj(1))
        bg_ref[rows, :] = proj(2)
        cv_ref[rows, :] = proj(3) * proj(4)


def _conv_in(x, pre_g, w_in):
    t = x.shape[0]
    row = pl.BlockSpec((PROJ_TM, D_MODEL), lambda i: (i, 0))
    half = pl.BlockSpec((PROJ_TM, D_A), lambda i: (i, 0))
    out = jax.ShapeDtypeStruct((t, D_A), F32)
    return pl.pallas_call(
        _conv_in_kernel,
        grid=(t // PROJ_TM,),
        in_specs=[row, _const_spec((1, D_MODEL)), _const_spec((D_MODEL, 5 * D_A))],
        out_specs=[half, half, half],
        out_shape=[out, out, out],
        compiler_params=_params(1),
        name="conv_in",
    )(x, pre_g, w_in)


CONV_ROWS = 256
CONV_LANES = 128
NORM_ROWS = 32
NORM_UNROLL = 8
SUBLANES = 8
PAD_A = (CONV_A_WIDTH - 1) // 2
PAD_B = (CONV_B_WIDTH - 1) // 2
SHIFTS_B = sorted({(HALO_B + k - PAD_B) % SUBLANES for k in range(CONV_B_WIDTH)})


def _conv_mix_kernel(a_ref, ap_ref, an_ref, cv_ref, cp_ref, cn_ref, bg_ref,
                     dww_ref, dwb_ref, lng_ref, lnb_ref, scw_ref, m_ref, ash, csh, conv_ref):
    j = pl.program_id(1)
    first = j == 0
    last = j == pl.num_programs(1) - 1
    ash[0, 0:HALO_A, :] = jnp.where(first, 0.0, ap_ref[0])
    ash[0, HALO_A:HALO_A + TS, :] = a_ref[0]
    ash[0, HALO_A + TS:, :] = jnp.where(last, 0.0, an_ref[0])
    csh[0, 0:HALO_B, :] = jnp.where(first, 0.0, cp_ref[0])
    csh[0, HALO_B:HALO_B + TS, :] = cv_ref[0]
    csh[0, HALO_B + TS:, :] = jnp.where(last, 0.0, cn_ref[0])
    rows_a = TS + 2 * HALO_A - SUBLANES
    for r in range(1, SUBLANES):
        ash[r, 0:rows_a, :] = ash[0, r:r + rows_a, :]
    rows_b = TS + 2 * HALO_B - SUBLANES
    for i, r in enumerate(SHIFTS_B):
        if r:
            csh[i, 0:rows_b, :] = csh[0, r:r + rows_b, :]

    groups = CONV_ROWS // SUBLANES

    for cb in range(D_A // CONV_LANES):
        lanes = slice(cb * CONV_LANES, (cb + 1) * CONV_LANES)

        def conv_block(i, carry, lanes=lanes):
            r0 = pl.multiple_of(i * CONV_ROWS, CONV_ROWS)
            acc = None
            for r in range(SUBLANES):
                taps = [k for k in range(CONV_A_WIDTH) if (HALO_A + k - PAD_A) % SUBLANES == r]
                offs = [HALO_A + k - PAD_A - r for k in taps]
                lo = min(offs)
                window = ash[r, pl.ds(r0 + lo, max(offs) - lo + CONV_ROWS), lanes]
                for k, off in zip(taps, offs):
                    slab = window[off - lo:off - lo + CONV_ROWS]
                    term = slab.reshape(groups, SUBLANES, CONV_LANES) * dww_ref[k, :, lanes][None]
                    acc = term if acc is None else acc + term
            conv_ref[pl.ds(r0, CONV_ROWS), lanes] = acc.reshape(CONV_ROWS, CONV_LANES)
            return carry

        lax.fori_loop(0, TS // CONV_ROWS, conv_block, 0)

    def norm_block(r0):
        rows = pl.ds(r0, NORM_ROWS)
        acc = conv_ref[rows, :] + dwb_ref[...]
        mu = jnp.mean(acc, axis=-1, keepdims=True)
        xc = acc - mu
        y = xc * lax.rsqrt(jnp.mean(xc * xc, axis=-1, keepdims=True) + EPS)
        y = y * lng_ref[...] + lnb_ref[...]
        m_ref[0, rows, 0:D_A] = (y * jax.nn.sigmoid(y)).astype(BF16)

        accb = None
        for k in range(CONV_B_WIDTH):
            off = HALO_B + k - PAD_B
            slab = csh[SHIFTS_B.index(off % SUBLANES), pl.ds(r0 + off - off % SUBLANES, NORM_ROWS), :]
            term = slab.reshape(NORM_ROWS // SUBLANES, SUBLANES, D_B) * scw_ref[k][None]
            accb = term if accb is None else accb + term
        m_ref[0, rows, D_A:] = (bg_ref[0, rows, :] * accb.reshape(NORM_ROWS, D_B)).astype(BF16)

    def norm_step(i, carry):
        for u in range(NORM_UNROLL):
            norm_block(pl.multiple_of((i * NORM_UNROLL + u) * NORM_ROWS, NORM_ROWS))
        return carry

    lax.fori_loop(0, TS // (NORM_ROWS * NORM_UNROLL), norm_step, 0)


def _conv_mix(a, cv, bg, dw_w, dw_b, ln_g, ln_b, sc_w):
    b, s, _ = a.shape
    n_s = s // TS
    grid = (b, n_s)

    def main(width):
        return pl.BlockSpec((1, TS, width), lambda i, j: (i, j, 0))

    def prev(halo, width):
        per = TS // halo
        return pl.BlockSpec((1, halo, width), lambda i, j: (i, jnp.maximum(j * per - 1, 0), 0))

    def nxt(halo, width):
        per = TS // halo
        n_blocks = s // halo
        return pl.BlockSpec((1, halo, width),
                            lambda i, j: (i, jnp.minimum((j + 1) * per, n_blocks - 1), 0))

    return pl.pallas_call(
        _conv_mix_kernel,
        grid=grid,
        in_specs=[main(D_A), prev(HALO_A, D_A), nxt(HALO_A, D_A),
                  main(D_B), prev(HALO_B, D_B), nxt(HALO_B, D_B),
                  main(D_B),
                  _const_spec((CONV_A_WIDTH, SUBLANES, D_A)), _const_spec((1, D_A)),
                  _const_spec((1, D_A)), _const_spec((1, D_A)),
                  _const_spec((CONV_B_WIDTH, SUBLANES, D_B))],
        out_specs=main(D_A + D_B),
        out_shape=jax.ShapeDtypeStruct((b, s, D_A + D_B), BF16),
        scratch_shapes=[pltpu.VMEM((SUBLANES, TS + 2 * HALO_A, D_A), F32),
                        pltpu.VMEM((len(SHIFTS_B), TS + 2 * HALO_B, D_B), F32),
                        pltpu.VMEM((TS, D_A), F32)],
        compiler_params=_params(2),
        name="conv_mix",
    )(a, a, a, cv, cv, cv, bg,
      jnp.broadcast_to(dw_w[:, None, :], (CONV_A_WIDTH, SUBLANES, D_A)), dw_b, ln_g, ln_b,
      jnp.broadcast_to(sc_w[:, None, :], (CONV_B_WIDTH, SUBLANES, D_B)))


QD = GLA_HEADS * GLA_DK
VD = GLA_HEADS * GLA_DV


def _gla_in_kernel(x_ref, pre_ref, w_ref, wgate_ref, wa2_ref, ba2_ref,
                   q_ref, k_ref, v_ref, r_ref, la_ref):
    subs = _sub_tiles(PROJ_TM, PROJ_SUB)
    hidden = [_rms(x_ref[rows, :], pre_ref[...]).astype(BF16) for rows in subs]
    for rows, h in zip(subs, hidden):
        gate = _dot(h, wgate_ref[...]).astype(BF16)
        q_ref[rows, :] = _dot(h, w_ref[:, 0:QD]).astype(BF16)
        z = _dot(gate, wa2_ref[...]) + ba2_ref[...]
        k_ref[rows, :] = _dot(h, w_ref[:, QD:2 * QD]).astype(BF16)
        log_sig = jnp.minimum(z, 0.0) - jnp.log(1.0 + jnp.exp(-jnp.abs(z)))
        la_ref[rows, :] = log_sig * (1.0 / GATE_TAU)
        v_ref[rows, :] = _dot(h, w_ref[:, 2 * QD:2 * QD + VD]).astype(BF16)
        r_ref[rows, :] = _dot(h, w_ref[:, 2 * QD + VD:2 * QD + 2 * VD]).astype(BF16)


def _gla_in(x, pre_g, w_main, w_gate, wa2, ba2):
    t = x.shape[0]

    def rows(width):
        return pl.BlockSpec((PROJ_TM, width), lambda i: (i, 0))

    def out(width, dtype):
        return jax.ShapeDtypeStruct((t, width), dtype)

    return pl.pallas_call(
        _gla_in_kernel,
        grid=(t // PROJ_TM,),
        in_specs=[rows(D_MODEL), _const_spec((1, D_MODEL)),
                  _const_spec(w_main.shape), _const_spec((D_MODEL, GATE_PAD)),
                  _const_spec((GATE_PAD, 2 * QD)), _const_spec((1, 2 * QD))],
        out_specs=[rows(QD), rows(QD), rows(VD), rows(VD), rows(2 * QD)],
        out_shape=[out(QD, BF16), out(QD, BF16), out(VD, BF16), out(VD, BF16), out(2 * QD, F32)],
        compiler_params=_params(1),
        name="gla_in",
    )(x, pre_g, w_main, w_gate, wa2, ba2)


GLA_SUPER = 2 * CHUNK
GLA_GROUP = 16


def _gla_kernel(q_ref, k_ref, v_ref, laf_ref, lab_ref, r_ref, gn_ref, m_ref,
                qfb_ref, dec_ref, kv_ref, st_ref, oacc_ref):
    n_pairs = q_ref.shape[1] // GLA_SUPER
    row = lax.broadcasted_iota(jnp.int32, (CHUNK, GLA_SUPER), 0)
    col = lax.broadcasted_iota(jnp.int32, (CHUNK, GLA_SUPER), 1)
    in_a = col < CHUNK
    mask_a_fwd = in_a & (row >= col)
    mask_a_bwd = (~in_a) | (row <= col)
    mask_b_fwd = in_a | (row >= col - CHUNK)
    mask_b_bwd = (~in_a) & (row <= col - CHUNK)
    tri = (lax.broadcasted_iota(jnp.int32, (CHUNK, CHUNK), 0)
           >= lax.broadcasted_iota(jnp.int32, (CHUNK, CHUNK), 1)).astype(BF16)
    tri3 = jnp.concatenate([tri, tri, tri], axis=1)
    scale = GLA_DK ** -0.5
    nt = (((1,), (1,)), ((), ()))
    tn = (((0,), (0,)), ((), ()))

    def pair_rows(p):
        return pl.ds(pl.multiple_of(p * GLA_SUPER, GLA_SUPER), GLA_SUPER)

    def group(i):
        return [i * GLA_GROUP + g for g in range(GLA_GROUP)]

    def prefix_sums(la):
        hi = la.astype(BF16)
        r1 = la - hi.astype(F32)
        mid = r1.astype(BF16)
        lo = (r1 - mid.astype(F32)).astype(BF16)
        return _dot(tri3, jnp.concatenate([hi, mid, lo], axis=0))

    def decayed(q, k, lab, pre):
        cum_f = pre[:, :GLA_DK]
        pre_b = pre[:, GLA_DK:]
        tot_f = cum_f[CHUNK - 1:CHUNK, :]
        tot_b = pre_b[CHUNK - 1:CHUNK, :]
        cum_b = tot_b - pre_b + lab
        kf = k * jnp.exp(-cum_f)
        kb = k * jnp.exp(-cum_b)
        dec_f = jnp.exp(tot_f)
        dec_b = jnp.exp(tot_b)
        return dict(qf=q * jnp.exp(cum_f), qb=q * jnp.exp(cum_b), kf=kf, kb=kb,
                    kef=kf * dec_f, keb=kb * dec_b, dec_f=dec_f, dec_b=dec_b)

    def rows_cat(top, bottom):
        return jnp.concatenate([top.astype(BF16), bottom.astype(BF16)], axis=0)

    def front(i, carry):
        pairs = group(i)
        all_rows = [pair_rows(p) for p in pairs]
        las, pres = [], []
        for rows in all_rows:
            la = jnp.concatenate([laf_ref[0, rows, :], lab_ref[0, rows, :]], axis=1)
            las.append(la)
            pres.append((prefix_sums(la[:CHUNK]), prefix_sums(la[CHUNK:])))
        score_ops, kv_ops = [], []
        for p, rows, la, (pre_a, pre_b) in zip(pairs, all_rows, las, pres):
            q = q_ref[0, rows, :].astype(F32) * scale
            k = k_ref[0, rows, :].astype(F32)
            ca = decayed(q[:CHUNK], k[:CHUNK], la[:CHUNK, GLA_DK:], pre_a)
            cb = decayed(q[CHUNK:], k[CHUNK:], la[CHUNK:, GLA_DK:], pre_b)
            qfb_ref[rows, :GLA_DK] = rows_cat(ca["qf"], cb["qf"] * ca["dec_f"])
            qfb_ref[rows, GLA_DK:] = rows_cat(ca["qb"] * cb["dec_b"], cb["qb"])
            k_end = jnp.concatenate([rows_cat(ca["kef"] * cb["dec_f"], cb["kef"]),
                                     rows_cat(ca["keb"], cb["keb"] * ca["dec_b"])], axis=1)
            dec = jnp.concatenate([ca["dec_f"] * cb["dec_f"], ca["dec_b"] * cb["dec_b"]], axis=1)
            dec_ref[p] = jnp.broadcast_to(dec, dec_ref.shape[1:])
            kv_ops.append(k_end)
            score_ops.append((ca["qf"].astype(BF16), rows_cat(ca["kf"], cb["kf"]),
                              ca["qb"].astype(BF16), rows_cat(ca["kb"], cb["keb"]),
                              cb["qf"].astype(BF16), rows_cat(ca["kef"], cb["kf"]),
                              cb["qb"].astype(BF16), rows_cat(ca["kb"], cb["kb"])))
        scores = [[lax.dot_general(ops[2 * j], ops[2 * j + 1], nt, preferred_element_type=F32)
                   for j in range(4)] for ops in score_ops]
        for p, rows, k_end in zip(pairs, all_rows, kv_ops):
            kv_ref[p] = lax.dot_general(v_ref[0, rows, :], k_end, tn,
                                        preferred_element_type=F32)
        for rows, (a_f, a_b, b_f, b_b) in zip(all_rows, scores):
            top = jnp.where(mask_a_fwd, a_f, 0.0) + jnp.where(mask_a_bwd, a_b, 0.0)
            bottom = jnp.where(mask_b_fwd, b_f, 0.0) + jnp.where(mask_b_bwd, b_b, 0.0)
            oacc_ref[rows, :] = _dot(rows_cat(top, bottom), v_ref[0, rows, :])

    def scan(lanes, order):
        def step(i, state):
            p = order(i)
            st_ref[p, :, lanes] = state.astype(BF16)
            return state * dec_ref[p][0:1, lanes] + kv_ref[p][:, lanes]
        lax.fori_loop(0, n_pairs, step, jnp.zeros((GLA_DV, GLA_DK), F32))

    def inter(i, carry):
        for p in group(i):
            rows = pair_rows(p)
            o = oacc_ref[rows, :] + lax.dot_general(qfb_ref[rows, :], st_ref[p], nt,
                                                    preferred_element_type=F32)
            o = o * lax.rsqrt(jnp.mean(o * o, axis=-1, keepdims=True) + EPS) * gn_ref[...]
            r = r_ref[0, rows, :].astype(F32)
            m_ref[0, rows, :] = (o * (r * jax.nn.sigmoid(r))).astype(BF16)

    def run(body, trips):
        def step(i, carry):
            body(i, carry)
            return carry
        lax.fori_loop(0, trips, step, 0)

    n_groups = n_pairs // GLA_GROUP
    run(front, n_groups)
    scan(slice(0, GLA_DK), lambda i: i)
    scan(slice(GLA_DK, 2 * GLA_DK), lambda i: n_pairs - 1 - i)
    run(inter, n_groups)


def _gla(q, k, v, la, r, gn_g):
    b, s, _ = q.shape
    n_pairs = s // GLA_SUPER

    def head(width, first_block=0):
        return pl.BlockSpec((1, s, width), lambda i, h: (i, 0, first_block + h))

    return pl.pallas_call(
        _gla_kernel,
        grid=(b, GLA_HEADS),
        in_specs=[head(GLA_DK), head(GLA_DK), head(GLA_DV), head(GLA_DK),
                  head(GLA_DK, GLA_HEADS), head(GLA_DV),
                  pl.BlockSpec((1, GLA_DV), lambda i, h: (0, h))],
        out_specs=head(GLA_DV),
        out_shape=jax.ShapeDtypeStruct((b, s, VD), BF16),
        scratch_shapes=[pltpu.VMEM((s, 2 * GLA_DK), BF16),
                        pltpu.VMEM((n_pairs, 8, 2 * GLA_DK), F32),
                        pltpu.VMEM((n_pairs, GLA_DV, 2 * GLA_DK), F32),
                        pltpu.VMEM((n_pairs, GLA_DV, 2 * GLA_DK), BF16),
                        pltpu.VMEM((s, GLA_DV), F32)],
        compiler_params=_params(2),
        name="gla",
    )(q, k, v, la, la, r, gn_g)


def kernel(x, mix_pre_g, mix_post_g, ffn_pre_g, ffn_post_g, cv_w_in, cv_dw_w, cv_dw_b, cv_ln_g,
           cv_ln_b, cv_sc_w, cv_w_out, gla_w_in, gla_wa2_f, gla_ba2_f, gla_wa2_b, gla_ba2_b,
           gla_gn_g, gla_w_out, ffn_w_gu, ffn_w_down):
    b, s, d = x.shape
    t = b * s

    def vec(g):
        return g.reshape(1, -1)

    def mix_out_ffn(xt, m, w_out, layer):
        return _ffn(xt, m, w_out.astype(BF16), vec(mix_post_g[layer]), vec(ffn_pre_g[layer]),
                    ffn_w_gu, ffn_w_down, vec(ffn_post_g[layer]), layer)

    xt = x.reshape(t, d)

    a, cv, bg = _conv_in(xt, vec(mix_pre_g[0]), cv_w_in[0].astype(BF16))
    m = _conv_mix(a.reshape(b, s, D_A), cv.reshape(b, s, D_B), bg.reshape(b, s, D_B),
                  cv_dw_w[0], vec(cv_dw_b[0]), vec(cv_ln_g[0]), vec(cv_ln_b[0]), cv_sc_w[0])
    xt = mix_out_ffn(xt, m.reshape(t, d), cv_w_out[0], 0)

    w_in = gla_w_in[0]
    n_main = 2 * QD + 2 * VD
    w_gate = jnp.pad(w_in[:, n_main:], ((0, 0), (0, GATE_PAD - 2 * GATE_RANK))).astype(BF16)
    wa2 = jnp.zeros((GATE_PAD, 2 * QD), F32)
    wa2 = wa2.at[:GATE_RANK, :QD].set(gla_wa2_f[0])
    wa2 = wa2.at[GATE_RANK:2 * GATE_RANK, QD:].set(gla_wa2_b[0])
    ba2 = jnp.concatenate([gla_ba2_f[0], gla_ba2_b[0]]).reshape(1, 2 * QD)
    q, k, v, r, la = _gla_in(xt, vec(mix_pre_g[1]), w_in.astype(BF16), w_gate,
                             wa2.astype(BF16), ba2)
    m = _gla(q.reshape(b, s, QD), k.reshape(b, s, QD), v.reshape(b, s, VD),
             la.reshape(b, s, 2 * QD), r.reshape(b, s, VD), vec(gla_gn_g[0]))
    xt = mix_out_ffn(xt, m.reshape(t, d), gla_w_out[0], 1)
    return xt.reshape(b, s, d)
```

```python
import functools

import jax
import jax.numpy as jnp
from jax import lax
from jax.experimental import pallas as pl
from jax.experimental.pallas import tpu as pltpu

D_MODEL = 1024
D_A = 512
D_B = 512
CONV_A_WIDTH = 31
CONV_B_WIDTH = 3
GLA_HEADS = 4
GLA_DK = 128
GLA_DV = 256
GATE_RANK = 16
GATE_TAU = 16.0
CHUNK = 64
D_FF = 2816
EPS = 1e-6

HALO_A = 16
HALO_B = 8
GATE_PAD = 128

PROJ_TM = 1024
PROJ_SUB = 512
TS = 1024
VMEM_LIMIT = 56 * 1024 * 1024

F32 = jnp.float32
BF16 = jnp.bfloat16


def _rms(x, g):
    return x * lax.rsqrt(jnp.mean(x * x, axis=-1, keepdims=True) + EPS) * g


def _dot(a, b):
    return jnp.dot(a, b, preferred_element_type=F32)


def _const_spec(shape):
    zeros = (0,) * len(shape)
    return pl.BlockSpec(shape, lambda *_: zeros, pipeline_mode=pl.Buffered(1))


def _params(n_grid):
    return pltpu.CompilerParams(dimension_semantics=("arbitrary",) * n_grid,
                                vmem_limit_bytes=VMEM_LIMIT)


def _sub_tiles(tile, sub):
    return [slice(i * sub, (i + 1) * sub) for i in range(tile // sub)]


FF_CHUNK = 256
FF_TM = 1024
FF_SUB = 512
FF_GU_ROWS = 64
FF_D_ROWS = 256
FF_VMEM_LIMIT = 60 * 1024 * 1024


def _stream_to_bf16(sources, stores, stage, sem):
    copies = [pltpu.make_async_copy(src, stage.at[i % 2], sem.at[i % 2])
              for i, src in enumerate(sources)]
    copies[0].start()
    for i, (copy, store) in enumerate(zip(copies, stores)):
        if i + 1 < len(copies):
            copies[i + 1].start()
        copy.wait()
        store(stage[i % 2].astype(BF16))


def _ffn_kernel(x_ref, m_ref, wo_ref, mpost_ref, pre_ref, wgu_hbm, wd_hbm, post_ref, o_ref,
                act_ref, wgu_ref, wd_ref, stage_gu, stage_d, sem_gu, sem_d, *, layer):
    @pl.when(pl.program_id(0) == 0)
    def _():
        def row_store(dst, rows, c):
            def store(block):
                dst[c * rows:(c + 1) * rows, :] = block
            return store

        n_gu = D_MODEL // FF_GU_ROWS
        n_d = D_FF // FF_D_ROWS
        _stream_to_bf16([wgu_hbm.at[layer, pl.ds(c * FF_GU_ROWS, FF_GU_ROWS), :] for c in range(n_gu)],
                        [row_store(wgu_ref, FF_GU_ROWS, c) for c in range(n_gu)], stage_gu, sem_gu)
        _stream_to_bf16([wd_hbm.at[layer, pl.ds(c * FF_D_ROWS, FF_D_ROWS), :] for c in range(n_d)],
                        [row_store(wd_ref, FF_D_ROWS, c) for c in range(n_d)], stage_d, sem_d)

    subs = _sub_tiles(FF_TM, FF_SUB)
    mixed = [_dot(m_ref[rows, :], wo_ref[...]) for rows in subs]
    hidden = []
    for rows, mo in zip(subs, mixed):
        x = x_ref[rows, :] + _rms(mo, mpost_ref[...])
        o_ref[rows, :] = x
        hidden.append(_rms(x, pre_ref[...]).astype(BF16))
    for rows, h in zip(subs, hidden):
        for c in range(D_FF // FF_CHUNK):
            lo = c * FF_CHUNK
            g = _dot(h, wgu_ref[:, lo:lo + FF_CHUNK])
            u = _dot(h, wgu_ref[:, D_FF + lo:D_FF + lo + FF_CHUNK])
            act_ref[rows, lo:lo + FF_CHUNK] = (g * jax.nn.sigmoid(g) * u).astype(BF16)
    down = [_dot(act_ref[rows, :], wd_ref[...]) for rows in subs]
    for rows, f in zip(subs, down):
        o_ref[rows, :] = o_ref[rows, :] + _rms(f, post_ref[...])


def _ffn(x, m, w_out, mix_post_g, pre_g, w_gu_all, w_d_all, post_g, layer):
    t = x.shape[0]
    row = pl.BlockSpec((FF_TM, D_MODEL), lambda i: (i, 0))
    hbm = pl.BlockSpec(memory_space=pl.ANY)
    return pl.pallas_call(
        functools.partial(_ffn_kernel, layer=layer),
        grid=(t // FF_TM,),
        in_specs=[row, row, _const_spec((D_MODEL, D_MODEL)), _const_spec((1, D_MODEL)),
                  _const_spec((1, D_MODEL)), hbm, hbm, _const_spec((1, D_MODEL))],
        out_specs=row,
        out_shape=jax.ShapeDtypeStruct((t, D_MODEL), F32),
        scratch_shapes=[pltpu.VMEM((FF_TM, D_FF), BF16),
                        pltpu.VMEM((D_MODEL, 2 * D_FF), BF16),
                        pltpu.VMEM((D_FF, D_MODEL), BF16),
                        pltpu.VMEM((2, FF_GU_ROWS, 2 * D_FF), F32),
                        pltpu.VMEM((2, FF_D_ROWS, D_MODEL), F32),
                        pltpu.SemaphoreType.DMA((2,)),
                        pltpu.SemaphoreType.DMA((2,))],
        compiler_params=pltpu.CompilerParams(dimension_semantics=("arbitrary",),
                                             vmem_limit_bytes=FF_VMEM_LIMIT),
        name="ffn",
    )(x, m, w_out, mix_post_g, pre_g, w_gu_all, w_d_all, post_g)


def _conv_in_kernel(x_ref, pre_ref, w_ref, a_ref, cv_ref, bg_ref):
    subs = _sub_tiles(PROJ_TM, PROJ_SUB)
    hidden = [_rms(x_ref[rows, :], pre_ref[...]).astype(BF16) for rows in subs]
    for rows, h in zip(subs, hidden):
        def proj(j, h=h):
            return _dot(h, w_ref[:, j * D_A:(j + 1) * D_A])

        a_ref[rows, :] = proj(0) * jax.nn.sigmoid(proj(1))
        bg_ref[rows, :] = proj(2)
        cv_ref[rows, :] = proj(3) * proj(4)


def _conv_in(x, pre_g, w_in):
    t = x.shape[0]
    row = pl.BlockSpec((PROJ_TM, D_MODEL), lambda i: (i, 0))
    half = pl.BlockSpec((PROJ_TM, D_A), lambda i: (i, 0))
    out = jax.ShapeDtypeStruct((t, D_A), F32)
    return pl.pallas_call(
        _conv_in_kernel,
        grid=(t // PROJ_TM,),
        in_specs=[row, _const_spec((1, D_MODEL)), _const_spec((D_MODEL, 5 * D_A))],
        out_specs=[half, half, half],
        out_shape=[out, out, out],
        compiler_params=_params(1),
        name="conv_in",
    )(x, pre_g, w_in)


CONV_ROWS = 256
CONV_LANES = 128
NORM_ROWS = 32
NORM_UNROLL = 8
SUBLANES = 8
PAD_A = (CONV_A_WIDTH - 1) // 2
PAD_B = (CONV_B_WIDTH - 1) // 2
SHIFTS_B = sorted({(HALO_B + k - PAD_B) % SUBLANES for k in range(CONV_B_WIDTH)})


def _conv_mix_kernel(a_ref, ap_ref, an_ref, cv_ref, cp_ref, cn_ref, bg_ref,
                     dww_ref, dwb_ref, lng_ref, lnb_ref, scw_ref, m_ref, ash, csh, conv_ref):
    j = pl.program_id(1)
    first = j == 0
    last = j == pl.num_programs(1) - 1
    ash[0, 0:HALO_A, :] = jnp.where(first, 0.0, ap_ref[0])
    ash[0, HALO_A:HALO_A + TS, :] = a_ref[0]
    ash[0, HALO_A + TS:, :] = jnp.where(last, 0.0, an_ref[0])
    csh[0, 0:HALO_B, :] = jnp.where(first, 0.0, cp_ref[0])
    csh[0, HALO_B:HALO_B + TS, :] = cv_ref[0]
    csh[0, HALO_B + TS:, :] = jnp.where(last, 0.0, cn_ref[0])
    rows_a = TS + 2 * HALO_A - SUBLANES
    for r in range(1, SUBLANES):
        ash[r, 0:rows_a, :] = ash[0, r:r + rows_a, :]
    rows_b = TS + 2 * HALO_B - SUBLANES
    for i, r in enumerate(SHIFTS_B):
        if r:
            csh[i, 0:rows_b, :] = csh[0, r:r + rows_b, :]

    groups = CONV_ROWS // SUBLANES

    for cb in range(D_A // CONV_LANES):
        lanes = slice(cb * CONV_LANES, (cb + 1) * CONV_LANES)

        def conv_block(i, carry, lanes=lanes):
            r0 = pl.multiple_of(i * CONV_ROWS, CONV_ROWS)
            acc = None
            for r in range(SUBLANES):
                taps = [k for k in range(CONV_A_WIDTH) if (HALO_A + k - PAD_A) % SUBLANES == r]
                offs = [HALO_A + k - PAD_A - r for k in taps]
                lo = min(offs)
                window = ash[r, pl.ds(r0 + lo, max(offs) - lo + CONV_ROWS), lanes]
                for k, off in zip(taps, offs):
                    slab = window[off - lo:off - lo + CONV_ROWS]
                    term = slab.reshape(groups, SUBLANES, CONV_LANES) * dww_ref[k, :, lanes][None]
                    acc = term if acc is None else acc + term
            conv_ref[pl.ds(r0, CONV_ROWS), lanes] = acc.reshape(CONV_ROWS, CONV_LANES)
            return carry

        lax.fori_loop(0, TS // CONV_ROWS, conv_block, 0)

    def norm_block(r0):
        rows = pl.ds(r0, NORM_ROWS)
        acc = conv_ref[rows, :] + dwb_ref[...]
        mu = jnp.mean(acc, axis=-1, keepdims=True)
        xc = acc - mu
        y = xc * lax.rsqrt(jnp.mean(xc * xc, axis=-1, keepdims=True) + EPS)
        y = y * lng_ref[...] + lnb_ref[...]
        m_ref[0, rows, 0:D_A] = (y * jax.nn.sigmoid(y)).astype(BF16)

        accb = None
        for k in range(CONV_B_WIDTH):
            off = HALO_B + k - PAD_B
            slab = csh[SHIFTS_B.index(off % SUBLANES), pl.ds(r0 + off - off % SUBLANES, NORM_ROWS), :]
            term = slab.reshape(NORM_ROWS // SUBLANES, SUBLANES, D_B) * scw_ref[k][None]
            accb = term if accb is None else accb + term
        m_ref[0, rows, D_A:] = (bg_ref[0, rows, :] * accb.reshape(NORM_ROWS, D_B)).astype(BF16)

    def norm_step(i, carry):
        for u in range(NORM_UNROLL):
            norm_block(pl.multiple_of((i * NORM_UNROLL + u) * NORM_ROWS, NORM_ROWS))
        return carry

    lax.fori_loop(0, TS // (NORM_ROWS * NORM_UNROLL), norm_step, 0)


def _conv_mix(a, cv, bg, dw_w, dw_b, ln_g, ln_b, sc_w):
    b, s, _ = a.shape
    n_s = s // TS
    grid = (b, n_s)

    def main(width):
        return pl.BlockSpec((1, TS, width), lambda i, j: (i, j, 0))

    def prev(halo, width):
        per = TS // halo
        return pl.BlockSpec((1, halo, width), lambda i, j: (i, jnp.maximum(j * per - 1, 0), 0))

    def nxt(halo, width):
        per = TS // halo
        n_blocks = s // halo
        return pl.BlockSpec((1, halo, width),
                            lambda i, j: (i, jnp.minimum((j + 1) * per, n_blocks - 1), 0))

    return pl.pallas_call(
        _conv_mix_kernel,
        grid=grid,
        in_specs=[main(D_A), prev(HALO_A, D_A), nxt(HALO_A, D_A),
                  main(D_B), prev(HALO_B, D_B), nxt(HALO_B, D_B),
                  main(D_B),
                  _const_spec((CONV_A_WIDTH, SUBLANES, D_A)), _const_spec((1, D_A)),
                  _const_spec((1, D_A)), _const_spec((1, D_A)),
                  _const_spec((CONV_B_WIDTH, SUBLANES, D_B))],
        out_specs=main(D_A + D_B),
        out_shape=jax.ShapeDtypeStruct((b, s, D_A + D_B), BF16),
        scratch_shapes=[pltpu.VMEM((SUBLANES, TS + 2 * HALO_A, D_A), F32),
                        pltpu.VMEM((len(SHIFTS_B), TS + 2 * HALO_B, D_B), F32),
                        pltpu.VMEM((TS, D_A), F32)],
        compiler_params=_params(2),
        name="conv_mix",
    )(a, a, a, cv, cv, cv, bg,
      jnp.broadcast_to(dw_w[:, None, :], (CONV_A_WIDTH, SUBLANES, D_A)), dw_b, ln_g, ln_b,
      jnp.broadcast_to(sc_w[:, None, :], (CONV_B_WIDTH, SUBLANES, D_B)))


QD = GLA_HEADS * GLA_DK
VD = GLA_HEADS * GLA_DV


def _gla_in_kernel(x_ref, pre_ref, w_ref, wgate_ref, wa2_ref, ba2_ref,
                   q_ref, k_ref, v_ref, r_ref, la_ref):
    subs = _sub_tiles(PROJ_TM, PROJ_SUB)
    hidden = [_rms(x_ref[rows, :], pre_ref[...]).astype(BF16) for rows in subs]
    for rows, h in zip(subs, hidden):
        gate = _dot(h, wgate_ref[...]).astype(BF16)
        q_ref[rows, :] = _dot(h, w_ref[:, 0:QD]).astype(BF16)
        z = _dot(gate, wa2_ref[...]) + ba2_ref[...]
        k_ref[rows, :] = _dot(h, w_ref[:, QD:2 * QD]).astype(BF16)
        log_sig = jnp.minimum(z, 0.0) - jnp.log(1.0 + jnp.exp(-jnp.abs(z)))
        la_ref[rows, :] = log_sig * (1.0 / GATE_TAU)
        v_ref[rows, :] = _dot(h, w_ref[:, 2 * QD:2 * QD + VD]).astype(BF16)
        r_ref[rows, :] = _dot(h, w_ref[:, 2 * QD + VD:2 * QD + 2 * VD]).astype(BF16)


def _gla_in(x, pre_g, w_main, w_gate, wa2, ba2):
    t = x.shape[0]

    def rows(width):
        return pl.BlockSpec((PROJ_TM, width), lambda i: (i, 0))

    def out(width, dtype):
        return jax.ShapeDtypeStruct((t, width), dtype)

    return pl.pallas_call(
        _gla_in_kernel,
        grid=(t // PROJ_TM,),
        in_specs=[rows(D_MODEL), _const_spec((1, D_MODEL)),
                  _const_spec(w_main.shape), _const_spec((D_MODEL, GATE_PAD)),
                  _const_spec((GATE_PAD, 2 * QD)), _const_spec((1, 2 * QD))],
        out_specs=[rows(QD), rows(QD), rows(VD), rows(VD), rows(2 * QD)],
        out_shape=[out(QD, BF16), out(QD, BF16), out(VD, BF16), out(VD, BF16), out(2 * QD, F32)],
        compiler_params=_params(1),
        name="gla_in",
    )(x, pre_g, w_main, w_gate, wa2, ba2)


GLA_SUPER = 2 * CHUNK
GLA_GROUP = 16


def _gla_kernel(q_ref, k_ref, v_ref, laf_ref, lab_ref, r_ref, gn_ref, m_ref,
                qfb_ref, dec_ref, kv_ref, st_ref, oacc_ref):
    n_pairs = q_ref.shape[1] // GLA_SUPER
    row = lax.broadcasted_iota(jnp.int32, (CHUNK, GLA_SUPER), 0)
    col = lax.broadcasted_iota(jnp.int32, (CHUNK, GLA_SUPER), 1)
    in_a = col < CHUNK
    mask_a_fwd = in_a & (row >= col)
    mask_a_bwd = (~in_a) | (row <= col)
    mask_b_fwd = in_a | (row >= col - CHUNK)
    mask_b_bwd = (~in_a) & (row <= col - CHUNK)
    tri = (lax.broadcasted_iota(jnp.int32, (CHUNK, CHUNK), 0)
           >= lax.broadcasted_iota(jnp.int32, (CHUNK, CHUNK), 1)).astype(BF16)
    tri3 = jnp.concatenate([tri, tri, tri], axis=1)
    scale = GLA_DK ** -0.5
    nt = (((1,), (1,)), ((), ()))
    tn = (((0,), (0,)), ((), ()))

    def pair_rows(p):
        return pl.ds(pl.multiple_of(p * GLA_SUPER, GLA_SUPER), GLA_SUPER)

    def group(i):
        return [i * GLA_GROUP + g for g in range(GLA_GROUP)]

    def prefix_sums(la):
        hi = la.astype(BF16)
        r1 = la - hi.astype(F32)
        mid = r1.astype(BF16)
        lo = (r1 - mid.astype(F32)).astype(BF16)
        return _dot(tri3, jnp.concatenate([hi, mid, lo], axis=0))

    def decayed(q, k, lab, pre):
        cum_f = pre[:, :GLA_DK]
        pre_b = pre[:, GLA_DK:]
        tot_f = cum_f[CHUNK - 1:CHUNK, :]
        tot_b = pre_b[CHUNK - 1:CHUNK, :]
        cum_b = tot_b - pre_b + lab
        kf = k * jnp.exp(-cum_f)
        kb = k * jnp.exp(-cum_b)
        dec_f = jnp.exp(tot_f)
        dec_b = jnp.exp(tot_b)
        return dict(qf=q * jnp.exp(cum_f), qb=q * jnp.exp(cum_b), kf=kf, kb=kb,
                    kef=kf * dec_f, keb=kb * dec_b, dec_f=dec_f, dec_b=dec_b)

    def rows_cat(top, bottom):
        return jnp.concatenate([top.astype(BF16), bottom.astype(BF16)], axis=0)

    def front(i, carry):
        pairs = group(i)
        all_rows = [pair_rows(p) for p in pairs]
        las, pres = [], []
        for rows in all_rows:
            la = jnp.concatenate([laf_ref[0, rows, :], lab_ref[0, rows, :]], axis=1)
            las.append(la)
            pres.append((prefix_sums(la[:CHUNK]), prefix_sums(la[CHUNK:])))
        score_ops, kv_ops = [], []
        for p, rows, la, (pre_a, pre_b) in zip(pairs, all_rows, las, pres):
            q = q_ref[0, rows, :].astype(F32) * scale
            k = k_ref[0, rows, :].astype(F32)
            ca = decayed(q[:CHUNK], k[:CHUNK], la[:CHUNK, GLA_DK:], pre_a)
            cb = decayed(q[CHUNK:], k[CHUNK:], la[CHUNK:, GLA_DK:], pre_b)
            qfb_ref[rows, :GLA_DK] = rows_cat(ca["qf"], cb["qf"] * ca["dec_f"])
            qfb_ref[rows, GLA_DK:] = rows_cat(ca["qb"] * cb["dec_b"], cb["qb"])
            k_end = jnp.concatenate([rows_cat(ca["kef"] * cb["dec_f"], cb["kef"]),
                                     rows_cat(ca["keb"], cb["keb"] * ca["dec_b"])], axis=1)
            dec = jnp.concatenate([ca["dec_f"] * cb["dec_f"], ca["dec_b"] * cb["dec_b"]], axis=1)
            dec_ref[p] = jnp.broadcast_to(dec, dec_ref.shape[1:])
            kv_ops.append(k_end)
            score_ops.append((ca["qf"].astype(BF16), rows_cat(ca["kf"], cb["kf"]),
                              ca["qb"].astype(BF16), rows_cat(ca["kb"], cb["keb"]),
                              cb["qf"].astype(BF16), rows_cat(ca["kef"], cb["kf"]),
                              cb["qb"].astype(BF16), rows_cat(ca["kb"], cb["kb"])))
        scores = [[lax.dot_general(ops[2 * j], ops[2 * j + 1], nt, preferred_element_type=F32)
                   for j in range(4)] for ops in score_ops]
        for p, rows, k_end in zip(pairs, all_rows, kv_ops):
            kv_ref[p] = lax.dot_general(v_ref[0, rows, :], k_end, tn,
                                        preferred_element_type=F32)
        for rows, (a_f, a_b, b_f, b_b) in zip(all_rows, scores):
            top = jnp.where(mask_a_fwd, a_f, 0.0) + jnp.where(mask_a_bwd, a_b, 0.0)
            bottom = jnp.where(mask_b_fwd, b_f, 0.0) + jnp.where(mask_b_bwd, b_b, 0.0)
            oacc_ref[rows, :] = _dot(rows_cat(top, bottom), v_ref[0, rows, :])

    def scan(lanes, order):
        state = jnp.zeros((GLA_DV, GLA_DK), F32)
        for p in order:
            st_ref[p, :, lanes] = state.astype(BF16)
            state = state * dec_ref[p][0:1, lanes] + kv_ref[p][:, lanes]

    def inter(i, carry):
        for p in group(i):
            rows = pair_rows(p)
            o = oacc_ref[rows, :] + lax.dot_general(qfb_ref[rows, :], st_ref[p], nt,
                                                    preferred_element_type=F32)
            o = o * lax.rsqrt(jnp.mean(o * o, axis=-1, keepdims=True) + EPS) * gn_ref[...]
            r = r_ref[0, rows, :].astype(F32)
            m_ref[0, rows, :] = (o * (r * jax.nn.sigmoid(r))).astype(BF16)

    def run(body, trips):
        def step(i, carry):
            body(i, carry)
            return carry
        lax.fori_loop(0, trips, step, 0)

    n_groups = n_pairs // GLA_GROUP
    run(front, n_groups)
    scan(slice(0, GLA_DK), range(n_pairs))
    scan(slice(GLA_DK, 2 * GLA_DK), range(n_pairs - 1, -1, -1))
    run(inter, n_groups)


def _gla(q, k, v, la, r, gn_g):
    b, s, _ = q.shape
    n_pairs = s // GLA_SUPER

    def head(width, first_block=0):
        return pl.BlockSpec((1, s, width), lambda i, h: (i, 0, first_block + h))

    return pl.pallas_call(
        _gla_kernel,
        grid=(b, GLA_HEADS),
        in_specs=[head(GLA_DK), head(GLA_DK), head(GLA_DV), head(GLA_DK),
                  head(GLA_DK, GLA_HEADS), head(GLA_DV),
                  pl.BlockSpec((1, GLA_DV), lambda i, h: (0, h))],
        out_specs=head(GLA_DV),
        out_shape=jax.ShapeDtypeStruct((b, s, VD), BF16),
        scratch_shapes=[pltpu.VMEM((s, 2 * GLA_DK), BF16),
                        pltpu.VMEM((n_pairs, 8, 2 * GLA_DK), F32),
                        pltpu.VMEM((n_pairs, GLA_DV, 2 * GLA_DK), F32),
                        pltpu.VMEM((n_pairs, GLA_DV, 2 * GLA_DK), BF16),
                        pltpu.VMEM((s, GLA_DV), F32)],
        compiler_params=_params(2),
        name="gla",
    )(q, k, v, la, la, r, gn_g)


def kernel(x, mix_pre_g, mix_post_g, ffn_pre_g, ffn_post_g, cv_w_in, cv_dw_w, cv_dw_b, cv_ln_g,
           cv_ln_b, cv_sc_w, cv_w_out, gla_w_in, gla_wa2_f, gla_ba2_f, gla_wa2_b, gla_ba2_b,
           gla_gn_g, gla_w_out, ffn_w_gu, ffn_w_down):
    b, s, d = x.shape
    t = b * s

    def vec(g):
        return g.reshape(1, -1)

    def mix_out_ffn(xt, m, w_out, layer):
        return _ffn(xt, m, w_out.astype(BF16), vec(mix_post_g[layer]), vec(ffn_pre_g[layer]),
                    ffn_w_gu, ffn_w_down, vec(ffn_post_g[layer]), layer)

    xt = x.reshape(t, d)

    a, cv, bg = _conv_in(xt, vec(mix_pre_g[0]), cv_w_in[0].astype(BF16))
    m = _conv_mix(a.reshape(b, s, D_A), cv.reshape(b, s, D_B), bg.reshape(b, s, D_B),
                  cv_dw_w[0], vec(cv_dw_b[0]), vec(cv_ln_g[0]), vec(cv_ln_b[0]), cv_sc_w[0])
    xt = mix_out_ffn(xt, m.reshape(t, d), cv_w_out[0], 0)

    w_in = gla_w_in[0]
    n_main = 2 * QD + 2 * VD
    w_gate = jnp.pad(w_in[:, n_main:], ((0, 0), (0, GATE_PAD - 2 * GATE_RANK))).astype(BF16)
    wa2 = jnp.zeros((GATE_PAD, 2 * QD), F32)
    wa2 = wa2.at[:GATE_RANK, :QD].set(gla_wa2_f[0])
    wa2 = wa2.at[GATE_RANK:2 * GATE_RANK, QD:].set(gla_wa2_b[0])
    ba2 = jnp.concatenate([gla_ba2_f[0], gla_ba2_b[0]]).reshape(1, 2 * QD)
    q, k, v, r, la = _gla_in(xt, vec(mix_pre_g[1]), w_in.astype(BF16), w_gate,
                             wa2.astype(BF16), ba2)
    m = _gla(q.reshape(b, s, QD), k.reshape(b, s, QD), v.reshape(b, s, VD),
             la.reshape(b, s, 2 * QD), r.reshape(b, s, VD), vec(gla_gn_g[0]))
    xt = mix_out_ffn(xt, m.reshape(t, d), gla_w_out[0], 1)
    return xt.reshape(b, s, d)
```

```python
import functools

import jax
import jax.numpy as jnp
from jax import lax
from jax.experimental import pallas as pl
from jax.experimental.pallas import tpu as pltpu

D_MODEL = 1024
D_A = 512
D_B = 512
CONV_A_WIDTH = 31
CONV_B_WIDTH = 3
GLA_HEADS = 4
GLA_DK = 128
GLA_DV = 256
GATE_RANK = 16
GATE_TAU = 16.0
CHUNK = 64
D_FF = 2816
EPS = 1e-6

HALO_A = 16
HALO_B = 8
GATE_PAD = 128

PROJ_TM = 1024
PROJ_SUB = 512
TS = 1024
VMEM_LIMIT = 56 * 1024 * 1024

F32 = jnp.float32
BF16 = jnp.bfloat16


def _rms(x, g):
    return x * lax.rsqrt(jnp.mean(x * x, axis=-1, keepdims=True) + EPS) * g


def _dot(a, b):
    return jnp.dot(a, b, preferred_element_type=F32)


def _const_spec(shape):
    zeros = (0,) * len(shape)
    return pl.BlockSpec(shape, lambda *_: zeros, pipeline_mode=pl.Buffered(1))


def _params(n_grid):
    return pltpu.CompilerParams(dimension_semantics=("arbitrary",) * n_grid,
                                vmem_limit_bytes=VMEM_LIMIT)


def _sub_tiles(tile, sub):
    return [slice(i * sub, (i + 1) * sub) for i in range(tile // sub)]


FF_CHUNK = 256
FF_TM = 1024
FF_SUB = 512
FF_GU_ROWS = 64
FF_D_ROWS = 256
FF_VMEM_LIMIT = 60 * 1024 * 1024


def _stream_to_bf16(sources, stores, stage, sem):
    copies = [pltpu.make_async_copy(src, stage.at[i % 2], sem.at[i % 2])
              for i, src in enumerate(sources)]
    copies[0].start()
    for i, (copy, store) in enumerate(zip(copies, stores)):
        if i + 1 < len(copies):
            copies[i + 1].start()
        copy.wait()
        store(stage[i % 2].astype(BF16))


def _ffn_kernel(x_ref, m_ref, wo_ref, mpost_ref, pre_ref, wgu_hbm, wd_hbm, post_ref, o_ref,
                act_ref, wgu_ref, wd_ref, stage_gu, stage_d, sem_gu, sem_d, *, layer):
    @pl.when(pl.program_id(0) == 0)
    def _():
        def row_store(dst, rows, c):
            def store(block):
                dst[c * rows:(c + 1) * rows, :] = block
            return store

        n_gu = D_MODEL // FF_GU_ROWS
        n_d = D_FF // FF_D_ROWS
        _stream_to_bf16([wgu_hbm.at[layer, pl.ds(c * FF_GU_ROWS, FF_GU_ROWS), :] for c in range(n_gu)],
                        [row_store(wgu_ref, FF_GU_ROWS, c) for c in range(n_gu)], stage_gu, sem_gu)
        _stream_to_bf16([wd_hbm.at[layer, pl.ds(c * FF_D_ROWS, FF_D_ROWS), :] for c in range(n_d)],
                        [row_store(wd_ref, FF_D_ROWS, c) for c in range(n_d)], stage_d, sem_d)

    subs = _sub_tiles(FF_TM, FF_SUB)
    mixed = [_dot(m_ref[rows, :], wo_ref[...]) for rows in subs]
    hidden = []
    for rows, mo in zip(subs, mixed):
        x = x_ref[rows, :] + _rms(mo, mpost_ref[...])
        o_ref[rows, :] = x
        hidden.append(_rms(x, pre_ref[...]).astype(BF16))
    for rows, h in zip(subs, hidden):
        for c in range(D_FF // FF_CHUNK):
            lo = c * FF_CHUNK
            g = _dot(h, wgu_ref[:, lo:lo + FF_CHUNK])
            u = _dot(h, wgu_ref[:, D_FF + lo:D_FF + lo + FF_CHUNK])
            act_ref[rows, lo:lo + FF_CHUNK] = (g * jax.nn.sigmoid(g) * u).astype(BF16)
    down = [_dot(act_ref[rows, :], wd_ref[...]) for rows in subs]
    for rows, f in zip(subs, down):
        o_ref[rows, :] = o_ref[rows, :] + _rms(f, post_ref[...])


def _ffn(x, m, w_out, mix_post_g, pre_g, w_gu_all, w_d_all, post_g, layer):
    t = x.shape[0]
    row = pl.BlockSpec((FF_TM, D_MODEL), lambda i: (i, 0))
    hbm = pl.BlockSpec(memory_space=pl.ANY)
    return pl.pallas_call(
        functools.partial(_ffn_kernel, layer=layer),
        grid=(t // FF_TM,),
        in_specs=[row, row, _const_spec((D_MODEL, D_MODEL)), _const_spec((1, D_MODEL)),
                  _const_spec((1, D_MODEL)), hbm, hbm, _const_spec((1, D_MODEL))],
        out_specs=row,
        out_shape=jax.ShapeDtypeStruct((t, D_MODEL), F32),
        scratch_shapes=[pltpu.VMEM((FF_TM, D_FF), BF16),
                        pltpu.VMEM((D_MODEL, 2 * D_FF), BF16),
                        pltpu.VMEM((D_FF, D_MODEL), BF16),
                        pltpu.VMEM((2, FF_GU_ROWS, 2 * D_FF), F32),
                        pltpu.VMEM((2, FF_D_ROWS, D_MODEL), F32),
                        pltpu.SemaphoreType.DMA((2,)),
                        pltpu.SemaphoreType.DMA((2,))],
        compiler_params=pltpu.CompilerParams(dimension_semantics=("arbitrary",),
                                             vmem_limit_bytes=FF_VMEM_LIMIT),
        name="ffn",
    )(x, m, w_out, mix_post_g, pre_g, w_gu_all, w_d_all, post_g)


def _conv_in_kernel(x_ref, pre_ref, w_ref, a_ref, cv_ref, bg_ref):
    subs = _sub_tiles(PROJ_TM, PROJ_SUB)
    hidden = [_rms(x_ref[rows, :], pre_ref[...]).astype(BF16) for rows in subs]
    for rows, h in zip(subs, hidden):
        def proj(j, h=h):
            return _dot(h, w_ref[:, j * D_A:(j + 1) * D_A])

        a_ref[rows, :] = proj(0) * jax.nn.sigmoid(proj(1))
        bg_ref[rows, :] = proj(2)
        cv_ref[rows, :] = proj(3) * proj(4)


def _conv_in(x, pre_g, w_in):
    t = x.shape[0]
    row = pl.BlockSpec((PROJ_TM, D_MODEL), lambda i: (i, 0))
    half = pl.BlockSpec((PROJ_TM, D_A), lambda i: (i, 0))
    out = jax.ShapeDtypeStruct((t, D_A), F32)
    return pl.pallas_call(
        _conv_in_kernel,
        grid=(t // PROJ_TM,),
        in_specs=[row, _const_spec((1, D_MODEL)), _const_spec((D_MODEL, 5 * D_A))],
        out_specs=[half, half, half],
        out_shape=[out, out, out],
        compiler_params=_params(1),
        name="conv_in",
    )(x, pre_g, w_in)


CONV_ROWS = 256
CONV_LANES = 128
NORM_ROWS = 32
NORM_UNROLL = 8
SUBLANES = 8
PAD_A = (CONV_A_WIDTH - 1) // 2
PAD_B = (CONV_B_WIDTH - 1) // 2
SHIFTS_B = sorted({(HALO_B + k - PAD_B) % SUBLANES for k in range(CONV_B_WIDTH)})


def _conv_mix_kernel(a_ref, ap_ref, an_ref, cv_ref, cp_ref, cn_ref, bg_ref,
                     dww_ref, dwb_ref, lng_ref, lnb_ref, scw_ref, m_ref, ash, csh, conv_ref):
    j = pl.program_id(1)
    first = j == 0
    last = j == pl.num_programs(1) - 1
    ash[0, 0:HALO_A, :] = jnp.where(first, 0.0, ap_ref[0])
    ash[0, HALO_A:HALO_A + TS, :] = a_ref[0]
    ash[0, HALO_A + TS:, :] = jnp.where(last, 0.0, an_ref[0])
    csh[0, 0:HALO_B, :] = jnp.where(first, 0.0, cp_ref[0])
    csh[0, HALO_B:HALO_B + TS, :] = cv_ref[0]
    csh[0, HALO_B + TS:, :] = jnp.where(last, 0.0, cn_ref[0])
    rows_a = TS + 2 * HALO_A - SUBLANES
    for r in range(1, SUBLANES):
        ash[r, 0:rows_a, :] = ash[0, r:r + rows_a, :]
    rows_b = TS + 2 * HALO_B - SUBLANES
    for i, r in enumerate(SHIFTS_B):
        if r:
            csh[i, 0:rows_b, :] = csh[0, r:r + rows_b, :]

    groups = CONV_ROWS // SUBLANES

    for cb in range(D_A // CONV_LANES):
        lanes = slice(cb * CONV_LANES, (cb + 1) * CONV_LANES)
        for r0 in range(0, TS, CONV_ROWS):
            acc = None
            for r in range(SUBLANES):
                taps = [k for k in range(CONV_A_WIDTH) if (HALO_A + k - PAD_A) % SUBLANES == r]
                offs = [HALO_A + k - PAD_A - r for k in taps]
                lo = min(offs)
                window = ash[r, r0 + lo:r0 + max(offs) + CONV_ROWS, lanes]
                for k, off in zip(taps, offs):
                    slab = window[off - lo:off - lo + CONV_ROWS]
                    term = slab.reshape(groups, SUBLANES, CONV_LANES) * dww_ref[k, :, lanes][None]
                    acc = term if acc is None else acc + term
            conv_ref[r0:r0 + CONV_ROWS, lanes] = acc.reshape(CONV_ROWS, CONV_LANES)

    def norm_block(r0):
        rows = pl.ds(r0, NORM_ROWS)
        acc = conv_ref[rows, :] + dwb_ref[...]
        mu = jnp.mean(acc, axis=-1, keepdims=True)
        xc = acc - mu
        y = xc * lax.rsqrt(jnp.mean(xc * xc, axis=-1, keepdims=True) + EPS)
        y = y * lng_ref[...] + lnb_ref[...]
        m_ref[0, rows, 0:D_A] = (y * jax.nn.sigmoid(y)).astype(BF16)

        accb = None
        for k in range(CONV_B_WIDTH):
            off = HALO_B + k - PAD_B
            slab = csh[SHIFTS_B.index(off % SUBLANES), pl.ds(r0 + off - off % SUBLANES, NORM_ROWS), :]
            term = slab.reshape(NORM_ROWS // SUBLANES, SUBLANES, D_B) * scw_ref[k][None]
            accb = term if accb is None else accb + term
        m_ref[0, rows, D_A:] = (bg_ref[0, rows, :] * accb.reshape(NORM_ROWS, D_B)).astype(BF16)

    def norm_step(i, carry):
        for u in range(NORM_UNROLL):
            norm_block(pl.multiple_of((i * NORM_UNROLL + u) * NORM_ROWS, NORM_ROWS))
        return carry

    lax.fori_loop(0, TS // (NORM_ROWS * NORM_UNROLL), norm_step, 0)


def _conv_mix(a, cv, bg, dw_w, dw_b, ln_g, ln_b, sc_w):
    b, s, _ = a.shape
    n_s = s // TS
    grid = (b, n_s)

    def main(width):
        return pl.BlockSpec((1, TS, width), lambda i, j: (i, j, 0))

    def prev(halo, width):
        per = TS // halo
        return pl.BlockSpec((1, halo, width), lambda i, j: (i, jnp.maximum(j * per - 1, 0), 0))

    def nxt(halo, width):
        per = TS // halo
        n_blocks = s // halo
        return pl.BlockSpec((1, halo, width),
                            lambda i, j: (i, jnp.minimum((j + 1) * per, n_blocks - 1), 0))

    return pl.pallas_call(
        _conv_mix_kernel,
        grid=grid,
        in_specs=[main(D_A), prev(HALO_A, D_A), nxt(HALO_A, D_A),
                  main(D_B), prev(HALO_B, D_B), nxt(HALO_B, D_B),
                  main(D_B),
                  _const_spec((CONV_A_WIDTH, SUBLANES, D_A)), _const_spec((1, D_A)),
                  _const_spec((1, D_A)), _const_spec((1, D_A)),
                  _const_spec((CONV_B_WIDTH, SUBLANES, D_B))],
        out_specs=main(D_A + D_B),
        out_shape=jax.ShapeDtypeStruct((b, s, D_A + D_B), BF16),
        scratch_shapes=[pltpu.VMEM((SUBLANES, TS + 2 * HALO_A, D_A), F32),
                        pltpu.VMEM((len(SHIFTS_B), TS + 2 * HALO_B, D_B), F32),
                        pltpu.VMEM((TS, D_A), F32)],
        compiler_params=_params(2),
        name="conv_mix",
    )(a, a, a, cv, cv, cv, bg,
      jnp.broadcast_to(dw_w[:, None, :], (CONV_A_WIDTH, SUBLANES, D_A)), dw_b, ln_g, ln_b,
      jnp.broadcast_to(sc_w[:, None, :], (CONV_B_WIDTH, SUBLANES, D_B)))


QD = GLA_HEADS * GLA_DK
VD = GLA_HEADS * GLA_DV


def _gla_in_kernel(x_ref, pre_ref, w_ref, wgate_ref, wa2_ref, ba2_ref,
                   q_ref, k_ref, v_ref, r_ref, la_ref):
    subs = _sub_tiles(PROJ_TM, PROJ_SUB)
    hidden = [_rms(x_ref[rows, :], pre_ref[...]).astype(BF16) for rows in subs]
    for rows, h in zip(subs, hidden):
        gate = _dot(h, wgate_ref[...]).astype(BF16)
        q_ref[rows, :] = _dot(h, w_ref[:, 0:QD]).astype(BF16)
        z = _dot(gate, wa2_ref[...]) + ba2_ref[...]
        k_ref[rows, :] = _dot(h, w_ref[:, QD:2 * QD]).astype(BF16)
        log_sig = jnp.minimum(z, 0.0) - jnp.log(1.0 + jnp.exp(-jnp.abs(z)))
        la_ref[rows, :] = log_sig * (1.0 / GATE_TAU)
        v_ref[rows, :] = _dot(h, w_ref[:, 2 * QD:2 * QD + VD]).astype(BF16)
        r_ref[rows, :] = _dot(h, w_ref[:, 2 * QD + VD:2 * QD + 2 * VD]).astype(BF16)


def _gla_in(x, pre_g, w_main, w_gate, wa2, ba2):
    t = x.shape[0]

    def rows(width):
        return pl.BlockSpec((PROJ_TM, width), lambda i: (i, 0))

    def out(width, dtype):
        return jax.ShapeDtypeStruct((t, width), dtype)

    return pl.pallas_call(
        _gla_in_kernel,
        grid=(t // PROJ_TM,),
        in_specs=[rows(D_MODEL), _const_spec((1, D_MODEL)),
                  _const_spec(w_main.shape), _const_spec((D_MODEL, GATE_PAD)),
                  _const_spec((GATE_PAD, 2 * QD)), _const_spec((1, 2 * QD))],
        out_specs=[rows(QD), rows(QD), rows(VD), rows(VD), rows(2 * QD)],
        out_shape=[out(QD, BF16), out(QD, BF16), out(VD, BF16), out(VD, BF16), out(2 * QD, F32)],
        compiler_params=_params(1),
        name="gla_in",
    )(x, pre_g, w_main, w_gate, wa2, ba2)


GLA_SUPER = 2 * CHUNK
GLA_GROUP = 16


def _gla_kernel(q_ref, k_ref, v_ref, laf_ref, lab_ref, r_ref, gn_ref, m_ref,
                qfb_ref, dec_ref, kv_ref, st_ref, oacc_ref):
    n_pairs = q_ref.shape[1] // GLA_SUPER
    row = lax.broadcasted_iota(jnp.int32, (CHUNK, GLA_SUPER), 0)
    col = lax.broadcasted_iota(jnp.int32, (CHUNK, GLA_SUPER), 1)
    in_a = col < CHUNK
    mask_a_fwd = in_a & (row >= col)
    mask_a_bwd = (~in_a) | (row <= col)
    mask_b_fwd = in_a | (row >= col - CHUNK)
    mask_b_bwd = (~in_a) & (row <= col - CHUNK)
    tri = (lax.broadcasted_iota(jnp.int32, (CHUNK, CHUNK), 0)
           >= lax.broadcasted_iota(jnp.int32, (CHUNK, CHUNK), 1)).astype(BF16)
    tri3 = jnp.concatenate([tri, tri, tri], axis=1)
    scale = GLA_DK ** -0.5
    nt = (((1,), (1,)), ((), ()))
    tn = (((0,), (0,)), ((), ()))

    def pair_rows(p):
        return pl.ds(pl.multiple_of(p * GLA_SUPER, GLA_SUPER), GLA_SUPER)

    def group(i):
        return [i * GLA_GROUP + g for g in range(GLA_GROUP)]

    def prefix_sums(la):
        hi = la.astype(BF16)
        r1 = la - hi.astype(F32)
        mid = r1.astype(BF16)
        lo = (r1 - mid.astype(F32)).astype(BF16)
        return _dot(tri3, jnp.concatenate([hi, mid, lo], axis=0))

    def decayed(q, k, lab, pre):
        cum_f = pre[:, :GLA_DK]
        pre_b = pre[:, GLA_DK:]
        tot_f = cum_f[CHUNK - 1:CHUNK, :]
        tot_b = pre_b[CHUNK - 1:CHUNK, :]
        cum_b = tot_b - pre_b + lab
        kf = k * jnp.exp(-cum_f)
        kb = k * jnp.exp(-cum_b)
        dec_f = jnp.exp(tot_f)
        dec_b = jnp.exp(tot_b)
        return dict(qf=q * jnp.exp(cum_f), qb=q * jnp.exp(cum_b), kf=kf, kb=kb,
                    kef=kf * dec_f, keb=kb * dec_b, dec_f=dec_f, dec_b=dec_b)

    def rows_cat(top, bottom):
        return jnp.concatenate([top.astype(BF16), bottom.astype(BF16)], axis=0)

    def front(i, carry):
        pairs = group(i)
        all_rows = [pair_rows(p) for p in pairs]
        las, pres = [], []
        for rows in all_rows:
            la = jnp.concatenate([laf_ref[0, rows, :], lab_ref[0, rows, :]], axis=1)
            las.append(la)
            pres.append((prefix_sums(la[:CHUNK]), prefix_sums(la[CHUNK:])))
        score_ops, kv_ops = [], []
        for p, rows, la, (pre_a, pre_b) in zip(pairs, all_rows, las, pres):
            q = q_ref[0, rows, :].astype(F32) * scale
            k = k_ref[0, rows, :].astype(F32)
            ca = decayed(q[:CHUNK], k[:CHUNK], la[:CHUNK, GLA_DK:], pre_a)
            cb = decayed(q[CHUNK:], k[CHUNK:], la[CHUNK:, GLA_DK:], pre_b)
            qfb_ref[rows, :GLA_DK] = rows_cat(ca["qf"], cb["qf"] * ca["dec_f"])
            qfb_ref[rows, GLA_DK:] = rows_cat(ca["qb"] * cb["dec_b"], cb["qb"])
            k_end = jnp.concatenate([rows_cat(ca["kef"] * cb["dec_f"], cb["kef"]),
                                     rows_cat(ca["keb"], cb["keb"] * ca["dec_b"])], axis=1)
            dec = jnp.concatenate([ca["dec_f"] * cb["dec_f"], ca["dec_b"] * cb["dec_b"]], axis=1)
            dec_ref[p] = jnp.broadcast_to(dec, dec_ref.shape[1:])
            kv_ops.append(k_end)
            score_ops.append((ca["qf"].astype(BF16), rows_cat(ca["kf"], cb["kf"]),
                              ca["qb"].astype(BF16), rows_cat(ca["kb"], cb["keb"]),
                              cb["qf"].astype(BF16), rows_cat(ca["kef"], cb["kf"]),
                              cb["qb"].astype(BF16), rows_cat(ca["kb"], cb["kb"])))
        scores = [[lax.dot_general(ops[2 * j], ops[2 * j + 1], nt, preferred_element_type=F32)
                   for j in range(4)] for ops in score_ops]
        for p, rows, k_end in zip(pairs, all_rows, kv_ops):
            kv_ref[p] = lax.dot_general(v_ref[0, rows, :], k_end, tn,
                                        preferred_element_type=F32)
        for rows, (a_f, a_b, b_f, b_b) in zip(all_rows, scores):
            top = jnp.where(mask_a_fwd, a_f, 0.0) + jnp.where(mask_a_bwd, a_b, 0.0)
            bottom = jnp.where(mask_b_fwd, b_f, 0.0) + jnp.where(mask_b_bwd, b_b, 0.0)
            oacc_ref[rows, :] = _dot(rows_cat(top, bottom), v_ref[0, rows, :])

    def scan(lanes, order):
        state = jnp.zeros((GLA_DV, GLA_DK), F32)
        for p in order:
            st_ref[p, :, lanes] = state.astype(BF16)
            state = state * dec_ref[p][0:1, lanes] + kv_ref[p][:, lanes]

    def inter(i, carry):
        for p in group(i):
            rows = pair_rows(p)
            o = oacc_ref[rows, :] + lax.dot_general(qfb_ref[rows, :], st_ref[p], nt,
                                                    preferred_element_type=F32)
            o = o * lax.rsqrt(jnp.mean(o * o, axis=-1, keepdims=True) + EPS) * gn_ref[...]
            r = r_ref[0, rows, :].astype(F32)
            m_ref[0, rows, :] = (o * (r * jax.nn.sigmoid(r))).astype(BF16)

    def run(body, trips):
        def step(i, carry):
            body(i, carry)
            return carry
        lax.fori_loop(0, trips, step, 0)

    n_groups = n_pairs // GLA_GROUP
    run(front, n_groups)
    scan(slice(0, GLA_DK), range(n_pairs))
    scan(slice(GLA_DK, 2 * GLA_DK), range(n_pairs - 1, -1, -1))
    run(inter, n_groups)


def _gla(q, k, v, la, r, gn_g):
    b, s, _ = q.shape
    n_pairs = s // GLA_SUPER

    def head(width, first_block=0):
        return pl.BlockSpec((1, s, width), lambda i, h: (i, 0, first_block + h))

    return pl.pallas_call(
        _gla_kernel,
        grid=(b, GLA_HEADS),
        in_specs=[head(GLA_DK), head(GLA_DK), head(GLA_DV), head(GLA_DK),
                  head(GLA_DK, GLA_HEADS), head(GLA_DV),
                  pl.BlockSpec((1, GLA_DV), lambda i, h: (0, h))],
        out_specs=head(GLA_DV),
        out_shape=jax.ShapeDtypeStruct((b, s, VD), BF16),
        scratch_shapes=[pltpu.VMEM((s, 2 * GLA_DK), BF16),
                        pltpu.VMEM((n_pairs, 8, 2 * GLA_DK), F32),
                        pltpu.VMEM((n_pairs, GLA_DV, 2 * GLA_DK), F32),
                        pltpu.VMEM((n_pairs, GLA_DV, 2 * GLA_DK), BF16),
                        pltpu.VMEM((s, GLA_DV), F32)],
        compiler_params=_params(2),
        name="gla",
    )(q, k, v, la, la, r, gn_g)


def kernel(x, mix_pre_g, mix_post_g, ffn_pre_g, ffn_post_g, cv_w_in, cv_dw_w, cv_dw_b, cv_ln_g,
           cv_ln_b, cv_sc_w, cv_w_out, gla_w_in, gla_wa2_f, gla_ba2_f, gla_wa2_b, gla_ba2_b,
           gla_gn_g, gla_w_out, ffn_w_gu, ffn_w_down):
    b, s, d = x.shape
    t = b * s

    def vec(g):
        return g.reshape(1, -1)

    def mix_out_ffn(xt, m, w_out, layer):
        return _ffn(xt, m, w_out.astype(BF16), vec(mix_post_g[layer]), vec(ffn_pre_g[layer]),
                    ffn_w_gu, ffn_w_down, vec(ffn_post_g[layer]), layer)

    xt = x.reshape(t, d)

    a, cv, bg = _conv_in(xt, vec(mix_pre_g[0]), cv_w_in[0].astype(BF16))
    m = _conv_mix(a.reshape(b, s, D_A), cv.reshape(b, s, D_B), bg.reshape(b, s, D_B),
                  cv_dw_w[0], vec(cv_dw_b[0]), vec(cv_ln_g[0]), vec(cv_ln_b[0]), cv_sc_w[0])
    xt = mix_out_ffn(xt, m.reshape(t, d), cv_w_out[0], 0)

    w_in = gla_w_in[0]
    n_main = 2 * QD + 2 * VD
    w_gate = jnp.pad(w_in[:, n_main:], ((0, 0), (0, GATE_PAD - 2 * GATE_RANK))).astype(BF16)
    wa2 = jnp.zeros((GATE_PAD, 2 * QD), F32)
    wa2 = wa2.at[:GATE_RANK, :QD].set(gla_wa2_f[0])
    wa2 = wa2.at[GATE_RANK:2 * GATE_RANK, QD:].set(gla_wa2_b[0])
    ba2 = jnp.concatenate([gla_ba2_f[0], gla_ba2_b[0]]).reshape(1, 2 * QD)
    q, k, v, r, la = _gla_in(xt, vec(mix_pre_g[1]), w_in.astype(BF16), w_gate,
                             wa2.astype(BF16), ba2)
    m = _gla(q.reshape(b, s, QD), k.reshape(b, s, QD), v.reshape(b, s, VD),
             la.reshape(b, s, 2 * QD), r.reshape(b, s, VD), vec(gla_gn_g[0]))
    xt = mix_out_ffn(xt, m.reshape(t, d), gla_w_out[0], 1)
    return xt.reshape(b, s, d)
```

```python
import functools

import jax
import jax.numpy as jnp
from jax import lax
from jax.experimental import pallas as pl
from jax.experimental.pallas import tpu as pltpu

D_MODEL = 1024
D_A = 512
D_B = 512
CONV_A_WIDTH = 31
CONV_B_WIDTH = 3
GLA_HEADS = 4
GLA_DK = 128
GLA_DV = 256
GATE_RANK = 16
GATE_TAU = 16.0
CHUNK = 64
D_FF = 2816
EPS = 1e-6

HALO_A = 16
HALO_B = 8
GATE_PAD = 128

PROJ_TM = 1024
PROJ_SUB = 512
TS = 1024
VMEM_LIMIT = 56 * 1024 * 1024

F32 = jnp.float32
BF16 = jnp.bfloat16


def _rms(x, g):
    return x * lax.rsqrt(jnp.mean(x * x, axis=-1, keepdims=True) + EPS) * g


def _dot(a, b):
    return jnp.dot(a, b, preferred_element_type=F32)


def _const_spec(shape):
    zeros = (0,) * len(shape)
    return pl.BlockSpec(shape, lambda *_: zeros, pipeline_mode=pl.Buffered(1))


def _params(n_grid):
    return pltpu.CompilerParams(dimension_semantics=("arbitrary",) * n_grid,
                                vmem_limit_bytes=VMEM_LIMIT)


def _sub_tiles(tile, sub):
    return [slice(i * sub, (i + 1) * sub) for i in range(tile // sub)]


FF_CHUNK = 256
FF_TM = 1024
FF_SUB = 512
FF_GU_ROWS = 64
FF_D_ROWS = 256
FF_GU_SLOTS = 4
FF_D_SLOTS = 2
FF_DMA_LOOKAHEAD = 3
FF_VMEM_LIMIT = 60 * 1024 * 1024


def _stream_to_bf16(streams, lookahead):
    jobs = []
    for c in range(max(len(s[0]) for s in streams)):
        for sid, (sources, stores, stage, sem) in enumerate(streams):
            if c < len(sources):
                slot = c % stage.shape[0]
                copy = pltpu.make_async_copy(sources[c], stage.at[slot], sem.at[slot])
                jobs.append((copy, stores[c], stage, slot, (sid, slot)))
    last_user = {}

    def start(j):
        key = jobs[j][4]
        assert key not in last_user or last_user[key] <= j - lookahead, "staging slot still in use"
        last_user[key] = j
        jobs[j][0].start()

    for j in range(min(lookahead, len(jobs))):
        start(j)
    for i, (copy, store, stage, slot, _) in enumerate(jobs):
        copy.wait()
        store(stage[slot].astype(BF16))
        if i + lookahead < len(jobs):
            start(i + lookahead)


def _ffn_kernel(x_ref, m_ref, wo_ref, mpost_ref, pre_ref, wgu_hbm, wd_hbm, post_ref, o_ref,
                act_ref, wgu_ref, wd_ref, stage_gu, stage_d, sem_gu, sem_d, *, layer):
    @pl.when(pl.program_id(0) == 0)
    def _():
        def row_store(dst, rows, c):
            def store(block):
                dst[c * rows:(c + 1) * rows, :] = block
            return store

        n_gu = D_MODEL // FF_GU_ROWS
        n_d = D_FF // FF_D_ROWS
        _stream_to_bf16(
            [([wgu_hbm.at[layer, pl.ds(c * FF_GU_ROWS, FF_GU_ROWS), :] for c in range(n_gu)],
              [row_store(wgu_ref, FF_GU_ROWS, c) for c in range(n_gu)], stage_gu, sem_gu),
             ([wd_hbm.at[layer, pl.ds(c * FF_D_ROWS, FF_D_ROWS), :] for c in range(n_d)],
              [row_store(wd_ref, FF_D_ROWS, c) for c in range(n_d)], stage_d, sem_d)],
            FF_DMA_LOOKAHEAD)

    subs = _sub_tiles(FF_TM, FF_SUB)
    mixed = [_dot(m_ref[rows, :], wo_ref[...]) for rows in subs]
    hidden = []
    for rows, mo in zip(subs, mixed):
        x = x_ref[rows, :] + _rms(mo, mpost_ref[...])
        o_ref[rows, :] = x
        hidden.append(_rms(x, pre_ref[...]).astype(BF16))
    for rows, h in zip(subs, hidden):
        for c in range(D_FF // FF_CHUNK):
            lo = c * FF_CHUNK
            g = _dot(h, wgu_ref[:, lo:lo + FF_CHUNK])
            u = _dot(h, wgu_ref[:, D_FF + lo:D_FF + lo + FF_CHUNK])
            act_ref[rows, lo:lo + FF_CHUNK] = (g * jax.nn.sigmoid(g) * u).astype(BF16)
    down = [_dot(act_ref[rows, :], wd_ref[...]) for rows in subs]
    for rows, f in zip(subs, down):
        o_ref[rows, :] = o_ref[rows, :] + _rms(f, post_ref[...])


def _ffn(x, m, w_out, mix_post_g, pre_g, w_gu_all, w_d_all, post_g, layer):
    t = x.shape[0]
    row = pl.BlockSpec((FF_TM, D_MODEL), lambda i: (i, 0))
    hbm = pl.BlockSpec(memory_space=pl.ANY)
    return pl.pallas_call(
        functools.partial(_ffn_kernel, layer=layer),
        grid=(t // FF_TM,),
        in_specs=[row, row, _const_spec((D_MODEL, D_MODEL)), _const_spec((1, D_MODEL)),
                  _const_spec((1, D_MODEL)), hbm, hbm, _const_spec((1, D_MODEL))],
        out_specs=row,
        out_shape=jax.ShapeDtypeStruct((t, D_MODEL), F32),
        scratch_shapes=[pltpu.VMEM((FF_TM, D_FF), BF16),
                        pltpu.VMEM((D_MODEL, 2 * D_FF), BF16),
                        pltpu.VMEM((D_FF, D_MODEL), BF16),
                        pltpu.VMEM((FF_GU_SLOTS, FF_GU_ROWS, 2 * D_FF), F32),
                        pltpu.VMEM((FF_D_SLOTS, FF_D_ROWS, D_MODEL), F32),
                        pltpu.SemaphoreType.DMA((FF_GU_SLOTS,)),
                        pltpu.SemaphoreType.DMA((FF_D_SLOTS,))],
        compiler_params=pltpu.CompilerParams(dimension_semantics=("arbitrary",),
                                             vmem_limit_bytes=FF_VMEM_LIMIT),
        name="ffn",
    )(x, m, w_out, mix_post_g, pre_g, w_gu_all, w_d_all, post_g)


def _conv_in_kernel(x_ref, pre_ref, w_ref, a_ref, cv_ref, bg_ref):
    subs = _sub_tiles(PROJ_TM, PROJ_SUB)
    hidden = [_rms(x_ref[rows, :], pre_ref[...]).astype(BF16) for rows in subs]
    for rows, h in zip(subs, hidden):
        def proj(j, h=h):
            return _dot(h, w_ref[:, j * D_A:(j + 1) * D_A])

        a_ref[rows, :] = proj(0) * jax.nn.sigmoid(proj(1))
        bg_ref[rows, :] = proj(2)
        cv_ref[rows, :] = proj(3) * proj(4)


def _conv_in(x, pre_g, w_in):
    t = x.shape[0]
    row = pl.BlockSpec((PROJ_TM, D_MODEL), lambda i: (i, 0))
    half = pl.BlockSpec((PROJ_TM, D_A), lambda i: (i, 0))
    out = jax.ShapeDtypeStruct((t, D_A), F32)
    return pl.pallas_call(
        _conv_in_kernel,
        grid=(t // PROJ_TM,),
        in_specs=[row, _const_spec((1, D_MODEL)), _const_spec((D_MODEL, 5 * D_A))],
        out_specs=[half, half, half],
        out_shape=[out, out, out],
        compiler_params=_params(1),
        name="conv_in",
    )(x, pre_g, w_in)


CONV_ROWS = 256
CONV_LANES = 128
NORM_ROWS = 32
NORM_UNROLL = 8
SUBLANES = 8
PAD_A = (CONV_A_WIDTH - 1) // 2
PAD_B = (CONV_B_WIDTH - 1) // 2
SHIFTS_B = sorted({(HALO_B + k - PAD_B) % SUBLANES for k in range(CONV_B_WIDTH)})


def _conv_mix_kernel(a_ref, ap_ref, an_ref, cv_ref, cp_ref, cn_ref, bg_ref,
                     dww_ref, dwb_ref, lng_ref, lnb_ref, scw_ref, m_ref, ash, csh, conv_ref):
    j = pl.program_id(1)
    first = j == 0
    last = j == pl.num_programs(1) - 1
    ash[0, 0:HALO_A, :] = jnp.where(first, 0.0, ap_ref[0])
    ash[0, HALO_A:HALO_A + TS, :] = a_ref[0]
    ash[0, HALO_A + TS:, :] = jnp.where(last, 0.0, an_ref[0])
    csh[0, 0:HALO_B, :] = jnp.where(first, 0.0, cp_ref[0])
    csh[0, HALO_B:HALO_B + TS, :] = cv_ref[0]
    csh[0, HALO_B + TS:, :] = jnp.where(last, 0.0, cn_ref[0])
    rows_a = TS + 2 * HALO_A - SUBLANES
    for r in range(1, SUBLANES):
        ash[r, 0:rows_a, :] = ash[0, r:r + rows_a, :]
    rows_b = TS + 2 * HALO_B - SUBLANES
    for i, r in enumerate(SHIFTS_B):
        if r:
            csh[i, 0:rows_b, :] = csh[0, r:r + rows_b, :]

    groups = CONV_ROWS // SUBLANES

    for cb in range(D_A // CONV_LANES):
        lanes = slice(cb * CONV_LANES, (cb + 1) * CONV_LANES)
        for r0 in range(0, TS, CONV_ROWS):
            acc = None
            for r in range(SUBLANES):
                taps = [k for k in range(CONV_A_WIDTH) if (HALO_A + k - PAD_A) % SUBLANES == r]
                offs = [HALO_A + k - PAD_A - r for k in taps]
                lo = min(offs)
                window = ash[r, r0 + lo:r0 + max(offs) + CONV_ROWS, lanes]
                for k, off in zip(taps, offs):
                    slab = window[off - lo:off - lo + CONV_ROWS]
                    term = slab.reshape(groups, SUBLANES, CONV_LANES) * dww_ref[k, :, lanes][None]
                    acc = term if acc is None else acc + term
            conv_ref[r0:r0 + CONV_ROWS, lanes] = acc.reshape(CONV_ROWS, CONV_LANES)

    def norm_block(r0):
        rows = pl.ds(r0, NORM_ROWS)
        acc = conv_ref[rows, :] + dwb_ref[...]
        mu = jnp.mean(acc, axis=-1, keepdims=True)
        xc = acc - mu
        y = xc * lax.rsqrt(jnp.mean(xc * xc, axis=-1, keepdims=True) + EPS)
        y = y * lng_ref[...] + lnb_ref[...]
        m_ref[0, rows, 0:D_A] = (y * jax.nn.sigmoid(y)).astype(BF16)

        accb = None
        for k in range(CONV_B_WIDTH):
            off = HALO_B + k - PAD_B
            slab = csh[SHIFTS_B.index(off % SUBLANES), pl.ds(r0 + off - off % SUBLANES, NORM_ROWS), :]
            term = slab.reshape(NORM_ROWS // SUBLANES, SUBLANES, D_B) * scw_ref[k][None]
            accb = term if accb is None else accb + term
        m_ref[0, rows, D_A:] = (bg_ref[0, rows, :] * accb.reshape(NORM_ROWS, D_B)).astype(BF16)

    def norm_step(i, carry):
        for u in range(NORM_UNROLL):
            norm_block(pl.multiple_of((i * NORM_UNROLL + u) * NORM_ROWS, NORM_ROWS))
        return carry

    lax.fori_loop(0, TS // (NORM_ROWS * NORM_UNROLL), norm_step, 0)


def _conv_mix(a, cv, bg, dw_w, dw_b, ln_g, ln_b, sc_w):
    b, s, _ = a.shape
    n_s = s // TS
    grid = (b, n_s)

    def main(width):
        return pl.BlockSpec((1, TS, width), lambda i, j: (i, j, 0))

    def prev(halo, width):
        per = TS // halo
        return pl.BlockSpec((1, halo, width), lambda i, j: (i, jnp.maximum(j * per - 1, 0), 0))

    def nxt(halo, width):
        per = TS // halo
        n_blocks = s // halo
        return pl.BlockSpec((1, halo, width),
                            lambda i, j: (i, jnp.minimum((j + 1) * per, n_blocks - 1), 0))

    return pl.pallas_call(
        _conv_mix_kernel,
        grid=grid,
        in_specs=[main(D_A), prev(HALO_A, D_A), nxt(HALO_A, D_A),
                  main(D_B), prev(HALO_B, D_B), nxt(HALO_B, D_B),
                  main(D_B),
                  _const_spec((CONV_A_WIDTH, SUBLANES, D_A)), _const_spec((1, D_A)),
                  _const_spec((1, D_A)), _const_spec((1, D_A)),
                  _const_spec((CONV_B_WIDTH, SUBLANES, D_B))],
        out_specs=main(D_A + D_B),
        out_shape=jax.ShapeDtypeStruct((b, s, D_A + D_B), BF16),
        scratch_shapes=[pltpu.VMEM((SUBLANES, TS + 2 * HALO_A, D_A), F32),
                        pltpu.VMEM((len(SHIFTS_B), TS + 2 * HALO_B, D_B), F32),
                        pltpu.VMEM((TS, D_A), F32)],
        compiler_params=_params(2),
        name="conv_mix",
    )(a, a, a, cv, cv, cv, bg,
      jnp.broadcast_to(dw_w[:, None, :], (CONV_A_WIDTH, SUBLANES, D_A)), dw_b, ln_g, ln_b,
      jnp.broadcast_to(sc_w[:, None, :], (CONV_B_WIDTH, SUBLANES, D_B)))


QD = GLA_HEADS * GLA_DK
VD = GLA_HEADS * GLA_DV


def _gla_in_kernel(x_ref, pre_ref, w_ref, wgate_ref, wa2_ref, ba2_ref,
                   q_ref, k_ref, v_ref, r_ref, la_ref):
    subs = _sub_tiles(PROJ_TM, PROJ_SUB)
    hidden = [_rms(x_ref[rows, :], pre_ref[...]).astype(BF16) for rows in subs]
    for rows, h in zip(subs, hidden):
        gate = _dot(h, wgate_ref[...]).astype(BF16)
        q_ref[rows, :] = _dot(h, w_ref[:, 0:QD]).astype(BF16)
        z = _dot(gate, wa2_ref[...]) + ba2_ref[...]
        k_ref[rows, :] = _dot(h, w_ref[:, QD:2 * QD]).astype(BF16)
        log_sig = jnp.minimum(z, 0.0) - jnp.log(1.0 + jnp.exp(-jnp.abs(z)))
        la_ref[rows, :] = log_sig * (1.0 / GATE_TAU)
        v_ref[rows, :] = _dot(h, w_ref[:, 2 * QD:2 * QD + VD]).astype(BF16)
        r_ref[rows, :] = _dot(h, w_ref[:, 2 * QD + VD:2 * QD + 2 * VD]).astype(BF16)


def _gla_in(x, pre_g, w_main, w_gate, wa2, ba2):
    t = x.shape[0]

    def rows(width):
        return pl.BlockSpec((PROJ_TM, width), lambda i: (i, 0))

    def out(width, dtype):
        return jax.ShapeDtypeStruct((t, width), dtype)

    return pl.pallas_call(
        _gla_in_kernel,
        grid=(t // PROJ_TM,),
        in_specs=[rows(D_MODEL), _const_spec((1, D_MODEL)),
                  _const_spec(w_main.shape), _const_spec((D_MODEL, GATE_PAD)),
                  _const_spec((GATE_PAD, 2 * QD)), _const_spec((1, 2 * QD))],
        out_specs=[rows(QD), rows(QD), rows(VD), rows(VD), rows(2 * QD)],
        out_shape=[out(QD, BF16), out(QD, BF16), out(VD, BF16), out(VD, BF16), out(2 * QD, F32)],
        compiler_params=_params(1),
        name="gla_in",
    )(x, pre_g, w_main, w_gate, wa2, ba2)


GLA_SUPER = 2 * CHUNK
GLA_GROUP = 16


def _gla_kernel(q_ref, k_ref, v_ref, laf_ref, lab_ref, r_ref, gn_ref, m_ref,
                qfb_ref, dec_ref, kv_ref, st_ref, oacc_ref):
    n_pairs = q_ref.shape[1] // GLA_SUPER
    row = lax.broadcasted_iota(jnp.int32, (CHUNK, GLA_SUPER), 0)
    col = lax.broadcasted_iota(jnp.int32, (CHUNK, GLA_SUPER), 1)
    in_a = col < CHUNK
    mask_a_fwd = in_a & (row >= col)
    mask_a_bwd = (~in_a) | (row <= col)
    mask_b_fwd = in_a | (row >= col - CHUNK)
    mask_b_bwd = (~in_a) & (row <= col - CHUNK)
    tri = (lax.broadcasted_iota(jnp.int32, (CHUNK, CHUNK), 0)
           >= lax.broadcasted_iota(jnp.int32, (CHUNK, CHUNK), 1)).astype(BF16)
    tri3 = jnp.concatenate([tri, tri, tri], axis=1)
    scale = GLA_DK ** -0.5
    nt = (((1,), (1,)), ((), ()))
    tn = (((0,), (0,)), ((), ()))

    def pair_rows(p):
        return pl.ds(pl.multiple_of(p * GLA_SUPER, GLA_SUPER), GLA_SUPER)

    def group(i):
        return [i * GLA_GROUP + g for g in range(GLA_GROUP)]

    def prefix_sums(la):
        hi = la.astype(BF16)
        r1 = la - hi.astype(F32)
        mid = r1.astype(BF16)
        lo = (r1 - mid.astype(F32)).astype(BF16)
        return _dot(tri3, jnp.concatenate([hi, mid, lo], axis=0))

    def decayed(q, k, lab, pre):
        cum_f = pre[:, :GLA_DK]
        pre_b = pre[:, GLA_DK:]
        tot_f = cum_f[CHUNK - 1:CHUNK, :]
        tot_b = pre_b[CHUNK - 1:CHUNK, :]
        cum_b = tot_b - pre_b + lab
        kf = k * jnp.exp(-cum_f)
        kb = k * jnp.exp(-cum_b)
        dec_f = jnp.exp(tot_f)
        dec_b = jnp.exp(tot_b)
        return dict(qf=q * jnp.exp(cum_f), qb=q * jnp.exp(cum_b), kf=kf, kb=kb,
                    kef=kf * dec_f, keb=kb * dec_b, dec_f=dec_f, dec_b=dec_b)

    def rows_cat(top, bottom):
        return jnp.concatenate([top.astype(BF16), bottom.astype(BF16)], axis=0)

    def front(i, carry):
        pairs = group(i)
        all_rows = [pair_rows(p) for p in pairs]
        las, pres = [], []
        for rows in all_rows:
            la = jnp.concatenate([laf_ref[0, rows, :], lab_ref[0, rows, :]], axis=1)
            las.append(la)
            pres.append((prefix_sums(la[:CHUNK]), prefix_sums(la[CHUNK:])))
        score_ops, kv_ops = [], []
        for p, rows, la, (pre_a, pre_b) in zip(pairs, all_rows, las, pres):
            q = q_ref[0, rows, :].astype(F32) * scale
            k = k_ref[0, rows, :].astype(F32)
            ca = decayed(q[:CHUNK], k[:CHUNK], la[:CHUNK, GLA_DK:], pre_a)
            cb = decayed(q[CHUNK:], k[CHUNK:], la[CHUNK:, GLA_DK:], pre_b)
            qfb_ref[rows, :GLA_DK] = rows_cat(ca["qf"], cb["qf"] * ca["dec_f"])
            qfb_ref[rows, GLA_DK:] = rows_cat(ca["qb"] * cb["dec_b"], cb["qb"])
            k_end = jnp.concatenate([rows_cat(ca["kef"] * cb["dec_f"], cb["kef"]),
                                     rows_cat(ca["keb"], cb["keb"] * ca["dec_b"])], axis=1)
            dec = jnp.concatenate([ca["dec_f"] * cb["dec_f"], ca["dec_b"] * cb["dec_b"]], axis=1)
            dec_ref[p] = jnp.broadcast_to(dec, dec_ref.shape[1:])
            kv_ops.append(k_end)
            score_ops.append((ca["qf"].astype(BF16), rows_cat(ca["kf"], cb["kf"]),
                              ca["qb"].astype(BF16), rows_cat(ca["kb"], cb["keb"]),
                              cb["qf"].astype(BF16), rows_cat(ca["kef"], cb["kf"]),
                              cb["qb"].astype(BF16), rows_cat(ca["kb"], cb["kb"])))
        scores = [[lax.dot_general(ops[2 * j], ops[2 * j + 1], nt, preferred_element_type=F32)
                   for j in range(4)] for ops in score_ops]
        for p, rows, k_end in zip(pairs, all_rows, kv_ops):
            kv_ref[p] = lax.dot_general(v_ref[0, rows, :], k_end, tn,
                                        preferred_element_type=F32)
        for rows, (a_f, a_b, b_f, b_b) in zip(all_rows, scores):
            top = jnp.where(mask_a_fwd, a_f, 0.0) + jnp.where(mask_a_bwd, a_b, 0.0)
            bottom = jnp.where(mask_b_fwd, b_f, 0.0) + jnp.where(mask_b_bwd, b_b, 0.0)
            oacc_ref[rows, :] = _dot(rows_cat(top, bottom), v_ref[0, rows, :])

    def scan(lanes, order):
        state = jnp.zeros((GLA_DV, GLA_DK), F32)
        for p in order:
            st_ref[p, :, lanes] = state.astype(BF16)
            state = state * dec_ref[p][0:1, lanes] + kv_ref[p][:, lanes]

    def inter(i, carry):
        for p in group(i):
            rows = pair_rows(p)
            o = oacc_ref[rows, :] + lax.dot_general(qfb_ref[rows, :], st_ref[p], nt,
                                                    preferred_element_type=F32)
            o = o * lax.rsqrt(jnp.mean(o * o, axis=-1, keepdims=True) + EPS) * gn_ref[...]
            r = r_ref[0, rows, :].astype(F32)
            m_ref[0, rows, :] = (o * (r * jax.nn.sigmoid(r))).astype(BF16)

    def run(body, trips):
        def step(i, carry):
            body(i, carry)
            return carry
        lax.fori_loop(0, trips, step, 0)

    n_groups = n_pairs // GLA_GROUP
    run(front, n_groups)
    scan(slice(0, GLA_DK), range(n_pairs))
    scan(slice(GLA_DK, 2 * GLA_DK), range(n_pairs - 1, -1, -1))
    run(inter, n_groups)


def _gla(q, k, v, la, r, gn_g):
    b, s, _ = q.shape
    n_pairs = s // GLA_SUPER

    def head(width, first_block=0):
        return pl.BlockSpec((1, s, width), lambda i, h: (i, 0, first_block + h))

    return pl.pallas_call(
        _gla_kernel,
        grid=(b, GLA_HEADS),
        in_specs=[head(GLA_DK), head(GLA_DK), head(GLA_DV), head(GLA_DK),
                  head(GLA_DK, GLA_HEADS), head(GLA_DV),
                  pl.BlockSpec((1, GLA_DV), lambda i, h: (0, h))],
        out_specs=head(GLA_DV),
        out_shape=jax.ShapeDtypeStruct((b, s, VD), BF16),
        scratch_shapes=[pltpu.VMEM((s, 2 * GLA_DK), BF16),
                        pltpu.VMEM((n_pairs, SUBLANES, 2 * GLA_DK), F32),
                        pltpu.VMEM((n_pairs, GLA_DV, 2 * GLA_DK), F32),
                        pltpu.VMEM((n_pairs, GLA_DV, 2 * GLA_DK), BF16),
                        pltpu.VMEM((s, GLA_DV), F32)],
        compiler_params=_params(2),
        name="gla",
    )(q, k, v, la, la, r, gn_g)


def kernel(x, mix_pre_g, mix_post_g, ffn_pre_g, ffn_post_g, cv_w_in, cv_dw_w, cv_dw_b, cv_ln_g,
           cv_ln_b, cv_sc_w, cv_w_out, gla_w_in, gla_wa2_f, gla_ba2_f, gla_wa2_b, gla_ba2_b,
           gla_gn_g, gla_w_out, ffn_w_gu, ffn_w_down):
    b, s, d = x.shape
    t = b * s

    def vec(g):
        return g.reshape(1, -1)

    def mix_out_ffn(xt, m, w_out, layer):
        return _ffn(xt, m, w_out.astype(BF16), vec(mix_post_g[layer]), vec(ffn_pre_g[layer]),
                    ffn_w_gu, ffn_w_down, vec(ffn_post_g[layer]), layer)

    xt = x.reshape(t, d)

    a, cv, bg = _conv_in(xt, vec(mix_pre_g[0]), cv_w_in[0].astype(BF16))
    m = _conv_mix(a.reshape(b, s, D_A), cv.reshape(b, s, D_B), bg.reshape(b, s, D_B),
                  cv_dw_w[0], vec(cv_dw_b[0]), vec(cv_ln_g[0]), vec(cv_ln_b[0]), cv_sc_w[0])
    xt = mix_out_ffn(xt, m.reshape(t, d), cv_w_out[0], 0)

    w_in = gla_w_in[0]
    n_main = 2 * QD + 2 * VD
    w_gate = jnp.pad(w_in[:, n_main:], ((0, 0), (0, GATE_PAD - 2 * GATE_RANK))).astype(BF16)
    wa2 = jnp.zeros((GATE_PAD, 2 * QD), F32)
    wa2 = wa2.at[:GATE_RANK, :QD].set(gla_wa2_f[0])
    wa2 = wa2.at[GATE_RANK:2 * GATE_RANK, QD:].set(gla_wa2_b[0])
    ba2 = jnp.concatenate([gla_ba2_f[0], gla_ba2_b[0]]).reshape(1, 2 * QD)
    q, k, v, r, la = _gla_in(xt, vec(mix_pre_g[1]), w_in.astype(BF16), w_gate,
                             wa2.astype(BF16), ba2)
    m = _gla(q.reshape(b, s, QD), k.reshape(b, s, QD), v.reshape(b, s, VD),
             la.reshape(b, s, 2 * QD), r.reshape(b, s, VD), vec(gla_gn_g[0]))
    xt = mix_out_ffn(xt, m.reshape(t, d), gla_w_out[0], 1)
    return xt.reshape(b, s, d)
```

```python
import functools

import jax
import jax.numpy as jnp
from jax import lax
from jax.experimental import pallas as pl
from jax.experimental.pallas import tpu as pltpu

D_MODEL = 1024
D_A = 512
D_B = 512
CONV_A_WIDTH = 31
CONV_B_WIDTH = 3
GLA_HEADS = 4
GLA_DK = 128
GLA_DV = 256
GATE_RANK = 16
GATE_TAU = 16.0
CHUNK = 64
D_FF = 2816
EPS = 1e-6

HALO_A = 16
HALO_B = 8
GATE_PAD = 128

PROJ_TM = 1024
PROJ_SUB = 512
TS = 1024
VMEM_LIMIT = 56 * 1024 * 1024

F32 = jnp.float32
BF16 = jnp.bfloat16


def _rms(x, g):
    return x * lax.rsqrt(jnp.mean(x * x, axis=-1, keepdims=True) + EPS) * g


def _dot(a, b):
    return jnp.dot(a, b, preferred_element_type=F32)


def _const_spec(shape):
    zeros = (0,) * len(shape)
    return pl.BlockSpec(shape, lambda *_: zeros, pipeline_mode=pl.Buffered(1))


def _params(n_grid):
    return pltpu.CompilerParams(dimension_semantics=("arbitrary",) * n_grid,
                                vmem_limit_bytes=VMEM_LIMIT)


def _sub_tiles(tile, sub):
    return [slice(i * sub, (i + 1) * sub) for i in range(tile // sub)]


FF_CHUNK = 256
FF_TM = 1024
FF_SUB = 512
FF_GU_ROWS = 64
FF_D_ROWS = 256
FF_GU_SLOTS = 4
FF_D_SLOTS = 2
FF_DMA_LOOKAHEAD = 3
FF_VMEM_LIMIT = 60 * 1024 * 1024


def _stream_to_bf16(streams, lookahead):
    jobs = []
    for c in range(max(len(s[0]) for s in streams)):
        for sid, (sources, stores, stage, sem) in enumerate(streams):
            if c < len(sources):
                slot = c % stage.shape[0]
                copy = pltpu.make_async_copy(sources[c], stage.at[slot], sem.at[slot])
                jobs.append((copy, stores[c], stage, slot, (sid, slot)))
    last_user = {}

    def start(j):
        key = jobs[j][4]
        assert key not in last_user or last_user[key] <= j - lookahead, "staging slot still in use"
        last_user[key] = j
        jobs[j][0].start()

    for j in range(min(lookahead, len(jobs))):
        start(j)
    for i, (copy, store, stage, slot, _) in enumerate(jobs):
        copy.wait()
        store(stage[slot].astype(BF16))
        if i + lookahead < len(jobs):
            start(i + lookahead)


def _ffn_kernel(x_ref, m_ref, wo_ref, mpost_ref, pre_ref, wgu_hbm, wd_hbm, post_ref, o_ref,
                act_ref, wgu_ref, wd_ref, stage_gu, stage_d, sem_gu, sem_d, *, layer):
    @pl.when(pl.program_id(0) == 0)
    def _():
        def row_store(dst, rows, c):
            def store(block):
                dst[c * rows:(c + 1) * rows, :] = block
            return store

        n_gu = D_MODEL // FF_GU_ROWS
        n_d = D_FF // FF_D_ROWS
        _stream_to_bf16(
            [([wgu_hbm.at[layer, pl.ds(c * FF_GU_ROWS, FF_GU_ROWS), :] for c in range(n_gu)],
              [row_store(wgu_ref, FF_GU_ROWS, c) for c in range(n_gu)], stage_gu, sem_gu),
             ([wd_hbm.at[layer, pl.ds(c * FF_D_ROWS, FF_D_ROWS), :] for c in range(n_d)],
              [row_store(wd_ref, FF_D_ROWS, c) for c in range(n_d)], stage_d, sem_d)],
            FF_DMA_LOOKAHEAD)

    subs = _sub_tiles(FF_TM, FF_SUB)
    mixed = [_dot(m_ref[rows, :], wo_ref[...]) for rows in subs]
    hidden = []
    for rows, mo in zip(subs, mixed):
        x = x_ref[rows, :] + _rms(mo, mpost_ref[...])
        o_ref[rows, :] = x
        hidden.append(_rms(x, pre_ref[...]).astype(BF16))
    for rows, h in zip(subs, hidden):
        for c in range(D_FF // FF_CHUNK):
            lo = c * FF_CHUNK
            g = _dot(h, wgu_ref[:, lo:lo + FF_CHUNK])
            u = _dot(h, wgu_ref[:, D_FF + lo:D_FF + lo + FF_CHUNK])
            act_ref[rows, lo:lo + FF_CHUNK] = (g * jax.nn.sigmoid(g) * u).astype(BF16)
    down = [_dot(act_ref[rows, :], wd_ref[...]) for rows in subs]
    for rows, f in zip(subs, down):
        o_ref[rows, :] = o_ref[rows, :] + _rms(f, post_ref[...])


def _ffn(x, m, w_out, mix_post_g, pre_g, w_gu_all, w_d_all, post_g, layer):
    t = x.shape[0]
    row = pl.BlockSpec((FF_TM, D_MODEL), lambda i: (i, 0))
    hbm = pl.BlockSpec(memory_space=pl.ANY)
    return pl.pallas_call(
        functools.partial(_ffn_kernel, layer=layer),
        grid=(t // FF_TM,),
        in_specs=[row, row, _const_spec((D_MODEL, D_MODEL)), _const_spec((1, D_MODEL)),
                  _const_spec((1, D_MODEL)), hbm, hbm, _const_spec((1, D_MODEL))],
        out_specs=row,
        out_shape=jax.ShapeDtypeStruct((t, D_MODEL), F32),
        scratch_shapes=[pltpu.VMEM((FF_TM, D_FF), BF16),
                        pltpu.VMEM((D_MODEL, 2 * D_FF), BF16),
                        pltpu.VMEM((D_FF, D_MODEL), BF16),
                        pltpu.VMEM((FF_GU_SLOTS, FF_GU_ROWS, 2 * D_FF), F32),
                        pltpu.VMEM((FF_D_SLOTS, FF_D_ROWS, D_MODEL), F32),
                        pltpu.SemaphoreType.DMA((FF_GU_SLOTS,)),
                        pltpu.SemaphoreType.DMA((FF_D_SLOTS,))],
        compiler_params=pltpu.CompilerParams(dimension_semantics=("arbitrary",),
                                             vmem_limit_bytes=FF_VMEM_LIMIT),
        name="ffn",
    )(x, m, w_out, mix_post_g, pre_g, w_gu_all, w_d_all, post_g)


def _conv_in_kernel(x_ref, pre_ref, w_ref, a_ref, cv_ref, bg_ref):
    subs = _sub_tiles(PROJ_TM, PROJ_SUB)
    hidden = [_rms(x_ref[rows, :], pre_ref[...]).astype(BF16) for rows in subs]
    for rows, h in zip(subs, hidden):
        def proj(j, h=h):
            return _dot(h, w_ref[:, j * D_A:(j + 1) * D_A])

        a_ref[rows, :] = proj(0) * jax.nn.sigmoid(proj(1))
        bg_ref[rows, :] = proj(2)
        cv_ref[rows, :] = proj(3) * proj(4)


def _conv_in(x, pre_g, w_in):
    t = x.shape[0]
    row = pl.BlockSpec((PROJ_TM, D_MODEL), lambda i: (i, 0))
    half = pl.BlockSpec((PROJ_TM, D_A), lambda i: (i, 0))
    out = jax.ShapeDtypeStruct((t, D_A), F32)
    return pl.pallas_call(
        _conv_in_kernel,
        grid=(t // PROJ_TM,),
        in_specs=[row, _const_spec((1, D_MODEL)), _const_spec((D_MODEL, 5 * D_A))],
        out_specs=[half, half, half],
        out_shape=[out, out, out],
        compiler_params=_params(1),
        name="conv_in",
    )(x, pre_g, w_in)


CONV_ROWS = 256
CONV_LANES = 128
NORM_ROWS = 32
NORM_UNROLL = 8
SUBLANES = 8
PAD_A = (CONV_A_WIDTH - 1) // 2
PAD_B = (CONV_B_WIDTH - 1) // 2
SHIFTS_B = sorted({(HALO_B + k - PAD_B) % SUBLANES for k in range(CONV_B_WIDTH)})


def _conv_mix_kernel(a_ref, ap_ref, an_ref, cv_ref, cp_ref, cn_ref, bg_ref,
                     dww_ref, dwb_ref, lng_ref, lnb_ref, scw_ref, m_ref, ash, csh, conv_ref):
    j = pl.program_id(1)
    first = j == 0
    last = j == pl.num_programs(1) - 1
    ash[0, 0:HALO_A, :] = jnp.where(first, 0.0, ap_ref[0])
    ash[0, HALO_A:HALO_A + TS, :] = a_ref[0]
    ash[0, HALO_A + TS:, :] = jnp.where(last, 0.0, an_ref[0])
    csh[0, 0:HALO_B, :] = jnp.where(first, 0.0, cp_ref[0])
    csh[0, HALO_B:HALO_B + TS, :] = cv_ref[0]
    csh[0, HALO_B + TS:, :] = jnp.where(last, 0.0, cn_ref[0])
    rows_a = TS + 2 * HALO_A - SUBLANES
    for r in range(1, SUBLANES):
        ash[r, 0:rows_a, :] = ash[0, r:r + rows_a, :]
    rows_b = TS + 2 * HALO_B - SUBLANES
    for i, r in enumerate(SHIFTS_B):
        if r:
            csh[i, 0:rows_b, :] = csh[0, r:r + rows_b, :]

    groups = CONV_ROWS // SUBLANES

    for cb in range(D_A // CONV_LANES):
        lanes = slice(cb * CONV_LANES, (cb + 1) * CONV_LANES)
        for r0 in range(0, TS, CONV_ROWS):
            acc = None
            for r in range(SUBLANES):
                taps = [k for k in range(CONV_A_WIDTH) if (HALO_A + k - PAD_A) % SUBLANES == r]
                offs = [HALO_A + k - PAD_A - r for k in taps]
                lo = min(offs)
                window = ash[r, r0 + lo:r0 + max(offs) + CONV_ROWS, lanes]
                for k, off in zip(taps, offs):
                    slab = window[off - lo:off - lo + CONV_ROWS]
                    term = slab.reshape(groups, SUBLANES, CONV_LANES) * dww_ref[k, :, lanes][None]
                    acc = term if acc is None else acc + term
            conv_ref[r0:r0 + CONV_ROWS, lanes] = acc.reshape(CONV_ROWS, CONV_LANES)

    def norm_block(r0):
        rows = pl.ds(r0, NORM_ROWS)
        acc = conv_ref[rows, :] + dwb_ref[...]
        mu = jnp.mean(acc, axis=-1, keepdims=True)
        xc = acc - mu
        y = xc * lax.rsqrt(jnp.mean(xc * xc, axis=-1, keepdims=True) + EPS)
        y = y * lng_ref[...] + lnb_ref[...]
        m_ref[0, rows, 0:D_A] = (y * jax.nn.sigmoid(y)).astype(BF16)

        accb = None
        for k in range(CONV_B_WIDTH):
            off = HALO_B + k - PAD_B
            slab = csh[SHIFTS_B.index(off % SUBLANES), pl.ds(r0 + off - off % SUBLANES, NORM_ROWS), :]
            term = slab.reshape(NORM_ROWS // SUBLANES, SUBLANES, D_B) * scw_ref[k][None]
            accb = term if accb is None else accb + term
        m_ref[0, rows, D_A:] = (bg_ref[0, rows, :] * accb.reshape(NORM_ROWS, D_B)).astype(BF16)

    def norm_step(i, carry):
        for u in range(NORM_UNROLL):
            norm_block(pl.multiple_of((i * NORM_UNROLL + u) * NORM_ROWS, NORM_ROWS))
        return carry

    lax.fori_loop(0, TS // (NORM_ROWS * NORM_UNROLL), norm_step, 0)


def _conv_mix(a, cv, bg, dw_w, dw_b, ln_g, ln_b, sc_w):
    b, s, _ = a.shape
    n_s = s // TS
    grid = (b, n_s)

    def main(width):
        return pl.BlockSpec((1, TS, width), lambda i, j: (i, j, 0))

    def prev(halo, width):
        per = TS // halo
        return pl.BlockSpec((1, halo, width), lambda i, j: (i, jnp.maximum(j * per - 1, 0), 0))

    def nxt(halo, width):
        per = TS // halo
        n_blocks = s // halo
        return pl.BlockSpec((1, halo, width),
                            lambda i, j: (i, jnp.minimum((j + 1) * per, n_blocks - 1), 0))

    return pl.pallas_call(
        _conv_mix_kernel,
        grid=grid,
        in_specs=[main(D_A), prev(HALO_A, D_A), nxt(HALO_A, D_A),
                  main(D_B), prev(HALO_B, D_B), nxt(HALO_B, D_B),
                  main(D_B),
                  _const_spec((CONV_A_WIDTH, SUBLANES, D_A)), _const_spec((1, D_A)),
                  _const_spec((1, D_A)), _const_spec((1, D_A)),
                  _const_spec((CONV_B_WIDTH, SUBLANES, D_B))],
        out_specs=main(D_A + D_B),
        out_shape=jax.ShapeDtypeStruct((b, s, D_A + D_B), BF16),
        scratch_shapes=[pltpu.VMEM((SUBLANES, TS + 2 * HALO_A, D_A), F32),
                        pltpu.VMEM((len(SHIFTS_B), TS + 2 * HALO_B, D_B), F32),
                        pltpu.VMEM((TS, D_A), F32)],
        compiler_params=_params(2),
        name="conv_mix",
    )(a, a, a, cv, cv, cv, bg,
      jnp.broadcast_to(dw_w[:, None, :], (CONV_A_WIDTH, SUBLANES, D_A)), dw_b, ln_g, ln_b,
      jnp.broadcast_to(sc_w[:, None, :], (CONV_B_WIDTH, SUBLANES, D_B)))


QD = GLA_HEADS * GLA_DK
VD = GLA_HEADS * GLA_DV


def _gla_in_kernel(x_ref, pre_ref, w_ref, wgate_ref, wa2_ref, ba2_ref,
                   q_ref, k_ref, v_ref, r_ref, la_ref):
    subs = _sub_tiles(PROJ_TM, PROJ_SUB)
    hidden = [_rms(x_ref[rows, :], pre_ref[...]).astype(BF16) for rows in subs]
    for rows, h in zip(subs, hidden):
        gate = _dot(h, wgate_ref[...]).astype(BF16)
        q_ref[rows, :] = _dot(h, w_ref[:, 0:QD]).astype(BF16)
        z = _dot(gate, wa2_ref[...]) + ba2_ref[...]
        k_ref[rows, :] = _dot(h, w_ref[:, QD:2 * QD]).astype(BF16)
        log_sig = jnp.minimum(z, 0.0) - jnp.log(1.0 + jnp.exp(-jnp.abs(z)))
        la_ref[rows, :] = log_sig * (1.0 / GATE_TAU)
        v_ref[rows, :] = _dot(h, w_ref[:, 2 * QD:2 * QD + VD]).astype(BF16)
        r_ref[rows, :] = _dot(h, w_ref[:, 2 * QD + VD:2 * QD + 2 * VD]).astype(BF16)


def _gla_in(x, pre_g, w_main, w_gate, wa2, ba2):
    t = x.shape[0]

    def rows(width):
        return pl.BlockSpec((PROJ_TM, width), lambda i: (i, 0))

    def out(width, dtype):
        return jax.ShapeDtypeStruct((t, width), dtype)

    return pl.pallas_call(
        _gla_in_kernel,
        grid=(t // PROJ_TM,),
        in_specs=[rows(D_MODEL), _const_spec((1, D_MODEL)),
                  _const_spec(w_main.shape), _const_spec((D_MODEL, GATE_PAD)),
                  _const_spec((GATE_PAD, 2 * QD)), _const_spec((1, 2 * QD))],
        out_specs=[rows(QD), rows(QD), rows(VD), rows(VD), rows(2 * QD)],
        out_shape=[out(QD, BF16), out(QD, BF16), out(VD, BF16), out(VD, BF16), out(2 * QD, F32)],
        compiler_params=_params(1),
        name="gla_in",
    )(x, pre_g, w_main, w_gate, wa2, ba2)


GLA_SUPER = 2 * CHUNK
GLA_GROUP = 8


def _gla_kernel(q_ref, k_ref, v_ref, laf_ref, lab_ref, r_ref, gn_ref, m_ref,
                qfb_ref, dec_ref, kv_ref, st_ref, oacc_ref):
    n_pairs = q_ref.shape[1] // GLA_SUPER
    row = lax.broadcasted_iota(jnp.int32, (CHUNK, GLA_SUPER), 0)
    col = lax.broadcasted_iota(jnp.int32, (CHUNK, GLA_SUPER), 1)
    in_a = col < CHUNK
    mask_a_fwd = in_a & (row >= col)
    mask_a_bwd = (~in_a) | (row <= col)
    mask_b_fwd = in_a | (row >= col - CHUNK)
    mask_b_bwd = (~in_a) & (row <= col - CHUNK)
    tri = (lax.broadcasted_iota(jnp.int32, (CHUNK, CHUNK), 0)
           >= lax.broadcasted_iota(jnp.int32, (CHUNK, CHUNK), 1)).astype(BF16)
    tri3 = jnp.concatenate([tri, tri, tri], axis=1)
    scale = GLA_DK ** -0.5
    nt = (((1,), (1,)), ((), ()))
    tn = (((0,), (0,)), ((), ()))

    def pair_rows(p):
        return pl.ds(pl.multiple_of(p * GLA_SUPER, GLA_SUPER), GLA_SUPER)

    def group(i):
        return [i * GLA_GROUP + g for g in range(GLA_GROUP)]

    def prefix_sums(la):
        hi = la.astype(BF16)
        r1 = la - hi.astype(F32)
        mid = r1.astype(BF16)
        lo = (r1 - mid.astype(F32)).astype(BF16)
        return _dot(tri3, jnp.concatenate([hi, mid, lo], axis=0))

    def decayed(q, k, lab, pre):
        cum_f = pre[:, :GLA_DK]
        pre_b = pre[:, GLA_DK:]
        tot_f = cum_f[CHUNK - 1:CHUNK, :]
        tot_b = pre_b[CHUNK - 1:CHUNK, :]
        cum_b = tot_b - pre_b + lab
        kf = k * jnp.exp(-cum_f)
        kb = k * jnp.exp(-cum_b)
        dec_f = jnp.exp(tot_f)
        dec_b = jnp.exp(tot_b)
        return dict(qf=q * jnp.exp(cum_f), qb=q * jnp.exp(cum_b), kf=kf, kb=kb,
                    kef=kf * dec_f, keb=kb * dec_b, dec_f=dec_f, dec_b=dec_b)

    def rows_cat(top, bottom):
        return jnp.concatenate([top.astype(BF16), bottom.astype(BF16)], axis=0)

    def front(i, carry):
        pairs = group(i)
        all_rows = [pair_rows(p) for p in pairs]
        las, pres = [], []
        for rows in all_rows:
            la = jnp.concatenate([laf_ref[0, rows, :], lab_ref[0, rows, :]], axis=1)
            las.append(la)
            pres.append((prefix_sums(la[:CHUNK]), prefix_sums(la[CHUNK:])))
        score_ops, kv_ops = [], []
        for p, rows, la, (pre_a, pre_b) in zip(pairs, all_rows, las, pres):
            q = q_ref[0, rows, :].astype(F32) * scale
            k = k_ref[0, rows, :].astype(F32)
            ca = decayed(q[:CHUNK], k[:CHUNK], la[:CHUNK, GLA_DK:], pre_a)
            cb = decayed(q[CHUNK:], k[CHUNK:], la[CHUNK:, GLA_DK:], pre_b)
            qfb_ref[rows, :GLA_DK] = rows_cat(ca["qf"], cb["qf"] * ca["dec_f"])
            qfb_ref[rows, GLA_DK:] = rows_cat(ca["qb"] * cb["dec_b"], cb["qb"])
            k_end = jnp.concatenate([rows_cat(ca["kef"] * cb["dec_f"], cb["kef"]),
                                     rows_cat(ca["keb"], cb["keb"] * ca["dec_b"])], axis=1)
            dec = jnp.concatenate([ca["dec_f"] * cb["dec_f"], ca["dec_b"] * cb["dec_b"]], axis=1)
            dec_ref[p] = jnp.broadcast_to(dec, dec_ref.shape[1:])
            kv_ops.append(k_end)
            score_ops.append((ca["qf"].astype(BF16), rows_cat(ca["kf"], cb["kf"]),
                              ca["qb"].astype(BF16), rows_cat(ca["kb"], cb["keb"]),
                              cb["qf"].astype(BF16), rows_cat(ca["kef"], cb["kf"]),
                              cb["qb"].astype(BF16), rows_cat(ca["kb"], cb["kb"])))
        scores = [[lax.dot_general(ops[2 * j], ops[2 * j + 1], nt, preferred_element_type=F32)
                   for j in range(4)] for ops in score_ops]
        for p, rows, k_end in zip(pairs, all_rows, kv_ops):
            kv_ref[p] = lax.dot_general(v_ref[0, rows, :], k_end, tn,
                                        preferred_element_type=F32)
        for rows, (a_f, a_b, b_f, b_b) in zip(all_rows, scores):
            top = jnp.where(mask_a_fwd, a_f, 0.0) + jnp.where(mask_a_bwd, a_b, 0.0)
            bottom = jnp.where(mask_b_fwd, b_f, 0.0) + jnp.where(mask_b_bwd, b_b, 0.0)
            oacc_ref[rows, :] = _dot(rows_cat(top, bottom), v_ref[0, rows, :])

    def scan(lanes, order):
        state = jnp.zeros((GLA_DV, GLA_DK), F32)
        for p in order:
            st_ref[p, :, lanes] = state.astype(BF16)
            state = state * dec_ref[p][0:1, lanes] + kv_ref[p][:, lanes]

    def inter(i, carry):
        for p in group(i):
            rows = pair_rows(p)
            o = oacc_ref[rows, :] + lax.dot_general(qfb_ref[rows, :], st_ref[p], nt,
                                                    preferred_element_type=F32)
            o = o * lax.rsqrt(jnp.mean(o * o, axis=-1, keepdims=True) + EPS) * gn_ref[...]
            r = r_ref[0, rows, :].astype(F32)
            m_ref[0, rows, :] = (o * (r * jax.nn.sigmoid(r))).astype(BF16)

    n_groups = n_pairs // GLA_GROUP
    for i in range(n_groups):
        front(i, None)
    scan(slice(0, GLA_DK), range(n_pairs))
    scan(slice(GLA_DK, 2 * GLA_DK), range(n_pairs - 1, -1, -1))
    for i in range(n_groups):
        inter(i, None)


def _gla(q, k, v, la, r, gn_g):
    b, s, _ = q.shape
    n_pairs = s // GLA_SUPER

    def head(width, first_block=0):
        return pl.BlockSpec((1, s, width), lambda i, h: (i, 0, first_block + h))

    return pl.pallas_call(
        _gla_kernel,
        grid=(b, GLA_HEADS),
        in_specs=[head(GLA_DK), head(GLA_DK), head(GLA_DV), head(GLA_DK),
                  head(GLA_DK, GLA_HEADS), head(GLA_DV),
                  pl.BlockSpec((1, GLA_DV), lambda i, h: (0, h))],
        out_specs=head(GLA_DV),
        out_shape=jax.ShapeDtypeStruct((b, s, VD), BF16),
        scratch_shapes=[pltpu.VMEM((s, 2 * GLA_DK), BF16),
                        pltpu.VMEM((n_pairs, SUBLANES, 2 * GLA_DK), F32),
                        pltpu.VMEM((n_pairs, GLA_DV, 2 * GLA_DK), F32),
                        pltpu.VMEM((n_pairs, GLA_DV, 2 * GLA_DK), BF16),
                        pltpu.VMEM((s, GLA_DV), F32)],
        compiler_params=_params(2),
        name="gla",
    )(q, k, v, la, la, r, gn_g)


def kernel(x, mix_pre_g, mix_post_g, ffn_pre_g, ffn_post_g, cv_w_in, cv_dw_w, cv_dw_b, cv_ln_g,
           cv_ln_b, cv_sc_w, cv_w_out, gla_w_in, gla_wa2_f, gla_ba2_f, gla_wa2_b, gla_ba2_b,
           gla_gn_g, gla_w_out, ffn_w_gu, ffn_w_down):
    b, s, d = x.shape
    t = b * s

    def vec(g):
        return g.reshape(1, -1)

    def mix_out_ffn(xt, m, w_out, layer):
        return _ffn(xt, m, w_out.astype(BF16), vec(mix_post_g[layer]), vec(ffn_pre_g[layer]),
                    ffn_w_gu, ffn_w_down, vec(ffn_post_g[layer]), layer)

    xt = x.reshape(t, d)

    a, cv, bg = _conv_in(xt, vec(mix_pre_g[0]), cv_w_in[0].astype(BF16))
    m = _conv_mix(a.reshape(b, s, D_A), cv.reshape(b, s, D_B), bg.reshape(b, s, D_B),
                  cv_dw_w[0], vec(cv_dw_b[0]), vec(cv_ln_g[0]), vec(cv_ln_b[0]), cv_sc_w[0])
    xt = mix_out_ffn(xt, m.reshape(t, d), cv_w_out[0], 0)

    w_in = gla_w_in[0]
    n_main = 2 * QD + 2 * VD
    w_gate = jnp.pad(w_in[:, n_main:], ((0, 0), (0, GATE_PAD - 2 * GATE_RANK))).astype(BF16)
    wa2 = jnp.zeros((GATE_PAD, 2 * QD), F32)
    wa2 = wa2.at[:GATE_RANK, :QD].set(gla_wa2_f[0])
    wa2 = wa2.at[GATE_RANK:2 * GATE_RANK, QD:].set(gla_wa2_b[0])
    ba2 = jnp.concatenate([gla_ba2_f[0], gla_ba2_b[0]]).reshape(1, 2 * QD)
    q, k, v, r, la = _gla_in(xt, vec(mix_pre_g[1]), w_in.astype(BF16), w_gate,
                             wa2.astype(BF16), ba2)
    m = _gla(q.reshape(b, s, QD), k.reshape(b, s, QD), v.reshape(b, s, VD),
             la.reshape(b, s, 2 * QD), r.reshape(b, s, VD), vec(gla_gn_g[0]))
    xt = mix_out_ffn(xt, m.reshape(t, d), gla_w_out[0], 1)
    return xt.reshape(b, s, d)
```
